```python
import math, functools
import jax, jax.numpy as jnp
from jax import lax
import numpy as np

D_MODEL = 4096
BATCH = 4
SEQ = 2048
DEPTH = 4
DEC_BATCH = 8
DEC_SEQ = 1
PAST_LEN = 8192
PAGE_SIZE = 128

HEAD_DIM = 128
ATT_HEADS = 8
ATT_QK = ATT_HEADS * 2 * HEAD_DIM
ATT_V = ATT_HEADS * 2 * HEAD_DIM
CHUNK = 128
MLP_GROUPS = 8
MLP_CH = 2048
CONV_CH = 2048
CONV_W = 3
BRANCH = 2048
N_BRANCH = 3
D_FF = 11008
PLE_DIM = 256
ROPE_THETA = 10000.0
LN_EPS = 1e-5
QBLOCK = 128
DN_ALPHA = (2 * DEPTH) ** 0.25
DN_BETA = (8 * DEPTH) ** -0.25
IN_SPLITS = (ATT_QK, ATT_QK, ATT_V, MLP_CH, MLP_CH, CONV_CH, CONV_CH, CONV_CH, N_BRANCH * D_MODEL)
IN_COLS = sum(IN_SPLITS)

kernel_name = "hybrid_diffattn_gmlp_conv_decoder_step"


def layer_norm(x, g, b):
    xf = x.astype(jnp.float32)
    mu = jnp.mean(xf, axis=-1, keepdims=True)
    var = jnp.mean(jnp.square(xf - mu), axis=-1, keepdims=True)
    return ((xf - mu) * lax.rsqrt(var + LN_EPS) * g.astype(jnp.float32) + b.astype(jnp.float32)).astype(x.dtype)


def rms_norm(x, g):
    xf = x.astype(jnp.float32)
    return (xf * lax.rsqrt(jnp.mean(jnp.square(xf), axis=-1, keepdims=True) + LN_EPS) * g.astype(jnp.float32)).astype(x.dtype)


def swiglu(x, w_in, w_out):
    a, b = jnp.split(x @ w_in, 2, axis=-1)
    return (jax.nn.silu(a) * b) @ w_out


def rope(x, pos):
    half = HEAD_DIM // 2
    inv = ROPE_THETA ** (-jnp.arange(half, dtype=jnp.float32) * 2.0 / HEAD_DIM)
    ang = pos.astype(jnp.float32)[:, None] * inv[None, :]
    cos = jnp.cos(ang)[:, None, None, :]
    sin = jnp.sin(ang)[:, None, None, :]
    xf = x.astype(jnp.float32)
    x1, x2 = xf[..., :half], xf[..., half:]
    return jnp.concatenate([x1 * cos - x2 * sin, x2 * cos + x1 * sin], axis=-1).astype(x.dtype)


def diff_attn_core(q, k, v, q_pos, k_pos, lam):
    s = jnp.einsum('bqhcd,bkhcd->bchqk', q, k, preferred_element_type=jnp.float32) * (HEAD_DIM ** -0.5)
    mask = k_pos[None, :] <= q_pos[:, None]
    s = jnp.where(mask, s, jnp.finfo(jnp.float32).min)
    a = jax.nn.softmax(s, axis=-1)
    w = a[:, 0] - lam * a[:, 1]
    return jnp.einsum('bhqk,bkhe->bqhe', w.astype(v.dtype), v)


def prompt_attn(q, k, v, lam):
    B, L = q.shape[0], q.shape[1]
    nb = L // QBLOCK
    qb = q.reshape(B, nb, QBLOCK, ATT_HEADS, 2, HEAD_DIM).swapaxes(0, 1)
    k_pos = jnp.arange(L)

    def blk(args):
        qi, bi = args
        q_pos = bi * QBLOCK + jnp.arange(QBLOCK)
        return diff_attn_core(qi, k, v, q_pos, k_pos, lam)

    o = lax.map(blk, (qb, jnp.arange(nb)))
    return o.swapaxes(0, 1).reshape(B, L, ATT_HEADS, 2 * HEAD_DIM)


def sample_attn(q, k_new, v_new, lam, cache_k_l, cache_v_l, page_table):
    Bd, Ld = q.shape[0], q.shape[1]
    past = page_table.shape[1] * PAGE_SIZE
    kp = cache_k_l[page_table].reshape(Bd, past, ATT_HEADS, 2, HEAD_DIM)
    vp = cache_v_l[page_table].reshape(Bd, past, ATT_HEADS, 2 * HEAD_DIM)
    k_all = jnp.concatenate([kp.astype(k_new.dtype), k_new], axis=1)
    v_all = jnp.concatenate([vp.astype(v_new.dtype), v_new], axis=1)
    k_pos = jnp.arange(past + Ld)
    q_pos = past + jnp.arange(Ld)
    return diff_attn_core(q, k_all, v_all, q_pos, k_pos, lam)


def chunk_spatial(vn, w_s, b_s):
    B, L, _ = vn.shape
    n_chunks = -(-L // CHUNK)
    pad = n_chunks * CHUNK - L
    vp = jnp.pad(vn, ((0, 0), (0, pad), (0, 0))).reshape(B, n_chunks, CHUNK, MLP_GROUPS, MLP_CH // MLP_GROUPS)
    causal = jnp.tril(jnp.ones((CHUNK, CHUNK), dtype=bool))
    wm = jnp.where(causal[None], w_s, 0.0).astype(vp.dtype)
    out = jnp.einsum('gts,bnsgc->bntgc', wm, vp) + b_s.T.astype(vp.dtype)[None, None, :, :, None]
    return out.reshape(B, n_chunks * CHUNK, MLP_CH)[:, :L]


def short_conv(z, prev, w):
    L = z.shape[1]
    zp = jnp.concatenate([prev.astype(z.dtype), z], axis=1)
    y = sum(w[j] * zp[:, j:j + L] for j in range(CONV_W))
    return y, zp[:, -(CONV_W - 1):]


def decoder_layer(i, x, p, pos, conv_prev, attend, ln_g, ln_b, f1i, f1o, wmi, lamv, sublng,
                  mlng, mlnb, ws, bs, cw, wbr, wmo, f2i, f2o, pg, pp):
    B, L, _ = x.shape
    x = layer_norm(DN_ALPHA * x + 0.5 * swiglu(x, f1i, f1o), ln_g[0], ln_b[0])
    idx = np.cumsum(IN_SPLITS)[:-1].tolist()
    q, k, v, u, vg, cb, cc, ch, gates = jnp.split(x @ wmi, idx, axis=-1)
    q = rope(q.reshape(B, L, ATT_HEADS, 2, HEAD_DIM), pos)
    k = rope(k.reshape(B, L, ATT_HEADS, 2, HEAD_DIM), pos)
    v = v.reshape(B, L, ATT_HEADS, 2 * HEAD_DIM)
    lam_init = 0.8 - 0.6 * math.exp(-0.3 * i)
    lf = lamv.astype(jnp.float32)
    lam = jnp.exp(jnp.sum(lf[0] * lf[1])) - jnp.exp(jnp.sum(lf[2] * lf[3])) + lam_init
    o = attend(q, k, v, lam)
    y_att = (rms_norm(o, sublng) * (1.0 - lam_init)).reshape(B, L, ATT_V)
    u = jax.nn.gelu(u)
    vn = layer_norm(jax.nn.gelu(vg), mlng, mlnb)
    y_mlp = u * chunk_spatial(vn, ws, bs)
    zc, conv_state = short_conv(cc * ch, conv_prev, cw)
    y_conv = cb * zc
    ys = jnp.stack([y_att, y_mlp, y_conv], axis=2)
    yb = jnp.einsum('blnc,ncd->blnd', ys, wbr)
    g = jax.nn.sigmoid(gates.reshape(B, L, N_BRANCH, D_MODEL))
    mixed = jnp.sum(g * yb, axis=2) @ wmo
    x = layer_norm(DN_ALPHA * x + mixed, ln_g[1], ln_b[1])
    x = layer_norm(DN_ALPHA * x + 0.5 * swiglu(x, f2i, f2o), ln_g[2], ln_b[2])
    x = x + jax.nn.sigmoid(x @ pg) * (p @ pp)
    return x, k.reshape(B, L, ATT_HEADS, 2 * HEAD_DIM), v, conv_state, vn


def setup_inputs(seed: int = 0) -> dict:
    key = jax.random.key(seed)
    ks = jax.random.split(key, 32)
    f32 = jnp.float32
    n_pages = PAST_LEN // PAGE_SIZE
    n_used = DEC_BATCH * n_pages
    n_pool = n_used + -(-n_used // 4)

    def nrm(k, shape, scale):
        return jax.random.normal(k, shape, f32) * scale

    page_table = jax.random.permutation(ks[5], n_pool)[:n_used].reshape(DEC_BATCH, n_pages).astype(jnp.int32)
    return {
        "x_prompt": nrm(ks[0], (BATCH, SEQ, D_MODEL), 1.0),
        "x_sample": nrm(ks[1], (DEC_BATCH, DEC_SEQ, D_MODEL), 1.0),
        "cache_k": nrm(ks[2], (DEPTH, n_pool, PAGE_SIZE, ATT_HEADS, 2 * HEAD_DIM), 1.0),
        "cache_v": nrm(ks[3], (DEPTH, n_pool, PAGE_SIZE, ATT_HEADS, 2 * HEAD_DIM), 1.0),
        "state_conv": nrm(ks[4], (DEPTH, DEC_BATCH, CONV_W - 1, CONV_CH), 1.0),
        "page_table": page_table,
        "p_prompt": nrm(ks[6], (DEPTH, BATCH, SEQ, PLE_DIM), 1.0),
        "p_sample": nrm(ks[7], (DEPTH, DEC_BATCH, DEC_SEQ, PLE_DIM), 1.0),
        "ln_g": 1.0 + nrm(ks[8], (DEPTH, 3, D_MODEL), 0.02),
        "ln_b": nrm(ks[9], (DEPTH, 3, D_MODEL), 0.02),
        "ffn1_w_in": nrm(ks[10], (DEPTH, D_MODEL, 2 * D_FF), D_MODEL ** -0.5),
        "ffn1_w_out": nrm(ks[11], (DEPTH, D_FF, D_MODEL), D_FF ** -0.5 * DN_BETA),
        "w_mix_in": nrm(ks[12], (DEPTH, D_MODEL, IN_COLS), D_MODEL ** -0.5),
        "att_lambda": nrm(ks[13], (DEPTH, 4, HEAD_DIM), 0.1),
        "att_subln_g": 1.0 + nrm(ks[14], (DEPTH, 2 * HEAD_DIM), 0.02),
        "mlp_ln_g": 1.0 + nrm(ks[15], (DEPTH, MLP_CH), 0.02),
        "mlp_ln_b": nrm(ks[16], (DEPTH, MLP_CH), 0.02),
        "mlp_w_s": nrm(ks[17], (DEPTH, MLP_GROUPS, CHUNK, CHUNK), CHUNK ** -0.5),
        "mlp_b_s": 1.0 + nrm(ks[18], (DEPTH, MLP_GROUPS, CHUNK), 0.02),
        "conv_w": nrm(ks[19], (DEPTH, CONV_W, CONV_CH), CONV_W ** -0.5),
        "w_branch": nrm(ks[20], (DEPTH, N_BRANCH, BRANCH, D_MODEL), BRANCH ** -0.5),
        "w_mix_out": nrm(ks[21], (DEPTH, D_MODEL, D_MODEL), D_MODEL ** -0.5 * DN_BETA),
        "ffn2_w_in": nrm(ks[22], (DEPTH, D_MODEL, 2 * D_FF), D_MODEL ** -0.5),
        "ffn2_w_out": nrm(ks[23], (DEPTH, D_FF, D_MODEL), D_FF ** -0.5 * DN_BETA),
        "ple_w_gate": nrm(ks[24], (DEPTH, D_MODEL, D_MODEL), D_MODEL ** -0.5),
        "ple_w_proj": nrm(ks[25], (DEPTH, PLE_DIM, D_MODEL), PLE_DIM ** -0.5 * DN_BETA),
    }


def reference(x_prompt, x_sample, cache_k, cache_v, state_conv, page_table, p_prompt, p_sample,
              ln_g, ln_b, ffn1_w_in, ffn1_w_out, w_mix_in, att_lambda, att_subln_g, mlp_ln_g, mlp_ln_b,
              mlp_w_s, mlp_b_s, conv_w, w_branch, w_mix_out, ffn2_w_in, ffn2_w_out, ple_w_gate, ple_w_proj):
    weights = (ln_g, ln_b, ffn1_w_in, ffn1_w_out, w_mix_in, att_lambda, att_subln_g, mlp_ln_g, mlp_ln_b,
               mlp_w_s, mlp_b_s, conv_w, w_branch, w_mix_out, ffn2_w_in, ffn2_w_out, ple_w_gate, ple_w_proj)
    Bp, Lp = x_prompt.shape[0], x_prompt.shape[1]
    Ld = x_sample.shape[1]
    past = page_table.shape[1] * PAGE_SIZE
    pos_prompt = jnp.arange(Lp)
    pos_sample = past + jnp.arange(Ld)
    conv0 = jnp.zeros((Bp, CONV_W - 1, CONV_CH), x_prompt.dtype)

    xp, xs = x_prompt, x_sample
    kp_l, vp_l, cp_l, ks_l, vs_l, cs_l, chs_l = [], [], [], [], [], [], []
    for i in range(DEPTH):
        w_i = [w[i] for w in weights]
        xp, kp, vp, cp, _ = decoder_layer(i, xp, p_prompt[i], pos_prompt, conv0, prompt_attn, *w_i)
        att_s = functools.partial(sample_attn, cache_k_l=cache_k[i], cache_v_l=cache_v[i], page_table=page_table)
        xs, k_s, v_s, c_s, vn_s = decoder_layer(i, xs, p_sample[i], pos_sample, state_conv[i], att_s, *w_i)
        kp_l.append(kp); vp_l.append(vp); cp_l.append(cp)
        ks_l.append(k_s); vs_l.append(v_s); cs_l.append(c_s); chs_l.append(vn_s)

    return (xp, xs, jnp.stack(kp_l), jnp.stack(vp_l), jnp.stack(cp_l),
            jnp.stack(ks_l), jnp.stack(vs_l), jnp.stack(cs_l), jnp.stack(chs_l))
```

```python
import functools
import math

import jax
import jax.numpy as jnp
from jax import lax
from jax.experimental import pallas as pl
from jax.experimental.pallas import tpu as pltpu

F32 = jnp.float32
BF16 = jnp.bfloat16

HEAD_DIM = 128
CHUNK = 128
PAGE_SIZE = 128
MLP_GROUPS = 8
CONV_W = 3
N_BRANCH = 3
LN_EPS = 1e-5
ROPE_THETA = 10000.0

VMEM_LIMIT_BYTES = 56 * 1024 * 1024
LANE = 128
SUBLANE = 8


def _tile(n, pref, mult=LANE):
    t = (min(pref, n) // mult) * mult
    while t >= mult:
        if n % t == 0:
            return t
        t -= mult
    return n


def _params(*sem):
    return pltpu.CompilerParams(dimension_semantics=sem, vmem_limit_bytes=VMEM_LIMIT_BYTES)


def _sigmoid(x):
    return 1.0 / (1.0 + jnp.exp(-x))


def _gelu_tanh(x):
    c = math.sqrt(2.0 / math.pi)
    return 0.5 * x * (1.0 + jnp.tanh(c * (x + 0.044715 * (x * x * x))))


def _cast_weight(w_ref, wb_ref, t, rows_per_step):
    k = w_ref.shape[0]
    steps = k // rows_per_step

    def body(r, carry):
        rows = pl.ds(pl.multiple_of(r * rows_per_step, rows_per_step), rows_per_step)
        wb_ref[t, rows, :] = w_ref[rows, :].astype(BF16)
        return carry

    lax.fori_loop(0, steps, body, 0)


def _mm_body(*refs, n_w, n_x, n_o, epi):
    a_ref = refs[0]
    w_refs = refs[1:1 + n_w]
    x_refs = refs[1 + n_w:1 + n_w + n_x]
    o_refs = refs[1 + n_w + n_x:1 + n_w + n_x + n_o]
    wb_ref = refs[1 + n_w + n_x + n_o]

    @pl.when(pl.program_id(1) == 0)
    def _():
        for t in range(n_w):
            k = w_refs[t].shape[0]
            _cast_weight(w_refs[t], wb_ref, t, _tile(k, 256, SUBLANE))

    a = a_ref[...]
    accs = [jnp.dot(a, wb_ref[t], preferred_element_type=F32) for t in range(n_w)]
    epi(accs, x_refs, o_refs)


def _mm_call(a, ws, w_specs, extras, extra_specs, out_shapes, out_specs, epi, *, tm, tn, n_col_blocks):
    r, k = a.shape
    body = functools.partial(_mm_body, n_w=len(ws), n_x=len(extras), n_o=len(out_shapes), epi=epi)
    return pl.pallas_call(
        body,
        grid=(n_col_blocks, r // tm),
        in_specs=[pl.BlockSpec((tm, k), lambda j, i: (i, 0)), *w_specs, *extra_specs],
        out_specs=out_specs,
        out_shape=out_shapes,
        scratch_shapes=[pltpu.VMEM((len(ws), k, tn), BF16)],
        compiler_params=_params("arbitrary", "arbitrary"),
        name="mm" + epi.__name__[len("_epi"):],
    )(a, *ws, *extras)


def _row_tile(r):
    return _tile(r, 1024, SUBLANE)


def _w_spec(layer, k, tn, off_blocks):
    return pl.BlockSpec((None, k, tn), lambda j, i: (layer, 0, off_blocks + j))


def _out_spec(tm, tn):
    return pl.BlockSpec((tm, tn), lambda j, i: (i, j))


def _epi_swiglu(accs, x_refs, o_refs):
    a, b = accs
    o_refs[0][...] = (a * _sigmoid(a) * b).astype(o_refs[0].dtype)


def _epi_plain(accs, x_refs, o_refs):
    o_refs[0][...] = accs[0].astype(o_refs[0].dtype)


def _epi_gelu(accs, x_refs, o_refs):
    o_refs[0][...] = _gelu_tanh(accs[0]).astype(o_refs[0].dtype)


def _epi_sigmoid(accs, x_refs, o_refs):
    o_refs[0][...] = _sigmoid(accs[0]).astype(o_refs[0].dtype)


def _epi_rope(accs, x_refs, o_refs):
    cos = x_refs[0][...]
    sin = x_refs[1][...]
    acc = accs[0]
    for c in range(acc.shape[1] // HEAD_DIM):
        cols = slice(c * HEAD_DIM, (c + 1) * HEAD_DIM)
        x = acc[:, cols]
        o_refs[0][:, cols] = (x * cos + pltpu.roll(x, HEAD_DIM // 2, 1) * sin).astype(o_refs[0].dtype)


def _epi_ple(accs, x_refs, o_refs):
    x_ref, p_ref, pp_ref = x_refs
    proj = jnp.dot(p_ref[...].astype(BF16), pp_ref[...].astype(BF16), preferred_element_type=F32)
    out = x_ref[...] + _sigmoid(accs[0]) * proj
    o_refs[0][...] = out
    o_refs[1][...] = out.astype(BF16)


def _ffn_in(xb, w_in, layer):
    r, d = xb.shape
    f = w_in.shape[2] // 2
    tn = _tile(f, 256)
    tm = _row_tile(r)
    nb = f // tn
    (h,) = _mm_call(
        xb, [w_in, w_in], [_w_spec(layer, d, tn, 0), _w_spec(layer, d, tn, nb)], [], [],
        [jax.ShapeDtypeStruct((r, f), BF16)], [_out_spec(tm, tn)], _epi_swiglu,
        tm=tm, tn=tn, n_col_blocks=nb)
    return h


def _proj(xb, w, layer, col_off, n_cols, epi, out_dtype, extras=(), extra_specs=(), tn_pref=512, tm=None):
    r, k = xb.shape
    tn = _tile(math.gcd(n_cols, col_off), tn_pref)
    assert col_off % tn == 0 and n_cols % tn == 0
    tm = _row_tile(r) if tm is None else tm
    (out,) = _mm_call(
        xb, [w], [_w_spec(layer, k, tn, col_off // tn)], list(extras), list(extra_specs),
        [jax.ShapeDtypeStruct((r, n_cols), out_dtype)], [_out_spec(tm, tn)], epi,
        tm=tm, tn=tn, n_col_blocks=n_cols // tn)
    return out


def _ple(xb, x, p, w_gate, w_proj, layer):
    r, d = xb.shape
    e = p.shape[-1]
    tn = _tile(d, 512)
    tm = _tile(r, 512, SUBLANE)
    extras = [x, p, w_proj]
    extra_specs = [
        pl.BlockSpec((tm, tn), lambda j, i: (i, j)),
        pl.BlockSpec((None, tm, e), lambda j, i: (layer, i, 0)),
        pl.BlockSpec((None, e, tn), lambda j, i: (layer, 0, j)),
    ]
    return _mm_call(
        xb, [w_gate], [_w_spec(layer, d, tn, 0)], extras, extra_specs,
        [jax.ShapeDtypeStruct((r, d), F32), jax.ShapeDtypeStruct((r, d), BF16)],
        [_out_spec(tm, tn), _out_spec(tm, tn)], _epi_ple,
        tm=tm, tn=tn, n_col_blocks=d // tn)


def _ffn_out_body(h_ref, w_ref, o_ref):
    @pl.when(pl.program_id(1) == 0)
    def _():
        o_ref[...] = jnp.zeros(o_ref.shape, F32)

    h = h_ref[...]
    d = o_ref.shape[1]
    tc = _tile(d, 1024)
    for c in range(d // tc):
        cols = slice(c * tc, (c + 1) * tc)
        o_ref[:, cols] += jnp.dot(h, w_ref[:, cols].astype(BF16), preferred_element_type=F32)


def _ffn_out(h, w_out, layer):
    r, f = h.shape
    d = w_out.shape[2]
    tk = _tile(f, 256)
    tm = _row_tile(r)
    return pl.pallas_call(
        _ffn_out_body,
        grid=(r // tm, f // tk),
        in_specs=[pl.BlockSpec((tm, tk), lambda i, k: (i, k)),
                  pl.BlockSpec((None, tk, d), lambda i, k: (layer, k, 0))],
        out_specs=pl.BlockSpec((tm, d), lambda i, k: (i, 0)),
        out_shape=jax.ShapeDtypeStruct((r, d), F32),
        compiler_params=_params("arbitrary", "arbitrary"),
        name="ffn_out",
    )(h, w_out)


def _ln_body(x_ref, y_ref, g_ref, b_ref, of_ref, ob_ref, *, alpha, s, which):
    z = alpha * x_ref[...] + s * y_ref[...]
    mu = jnp.mean(z, axis=-1, keepdims=True)
    zc = z - mu
    var = jnp.mean(zc * zc, axis=-1, keepdims=True)
    out = zc * lax.rsqrt(var + LN_EPS) * g_ref[which:which + 1, :] + b_ref[which:which + 1, :]
    of_ref[...] = out
    ob_ref[...] = out.astype(BF16)


def _res_ln(x, y, ln_g, ln_b, layer, which, alpha, s):
    r, d = x.shape
    tm = _tile(r, 256, SUBLANE)
    n_ln = ln_g.shape[1]
    body = functools.partial(_ln_body, alpha=alpha, s=s, which=which)
    row = pl.BlockSpec((tm, d), lambda i: (i, 0))
    par = pl.BlockSpec((None, n_ln, d), lambda i: (layer, 0, 0))
    return pl.pallas_call(
        body,
        grid=(r // tm,),
        in_specs=[row, row, par, par],
        out_specs=[row, row],
        out_shape=[jax.ShapeDtypeStruct((r, d), F32), jax.ShapeDtypeStruct((r, d), BF16)],
        compiler_params=_params("arbitrary"),
        name="res_ln",
    )(x, y, ln_g, ln_b)


def _lambda_value(lam_ref, lam_init):
    lf = lam_ref[...]
    a = jnp.sum(lf[0:1, :] * lf[1:2, :], axis=-1, keepdims=True)
    b = jnp.sum(lf[2:3, :] * lf[3:4, :], axis=-1, keepdims=True)
    return jnp.exp(a) - jnp.exp(b) + lam_init


def _subln(o, g_ref, lam_init):
    ms = jnp.mean(o * o, axis=-1, keepdims=True)
    return o * lax.rsqrt(ms + LN_EPS) * g_ref[...] * (1.0 - lam_init)


def _attn_body(q_ref, k_ref, v_ref, lam_ref, g_ref, o_ref, kb_ref, vb_ref, *, tq, tk, lam_init):
    i = pl.program_id(2)
    seq = k_ref.shape[0]

    @pl.when(i == 0)
    def _():
        def body(r, carry):
            rows = pl.ds(pl.multiple_of(r * tk, tk), tk)
            kb_ref[rows, :] = k_ref[rows, :].astype(BF16)
            vb_ref[rows, :] = v_ref[rows, :].astype(BF16)
            return carry
        lax.fori_loop(0, seq // tk, body, 0)

    lam = _lambda_value(lam_ref, lam_init)
    q = q_ref[...]
    qs = (q[:, :HEAD_DIM].astype(BF16), q[:, HEAD_DIM:].astype(BF16))
    scale = HEAD_DIM ** -0.5
    neg = jnp.finfo(F32).min
    row = i * tq + lax.broadcasted_iota(jnp.int32, (tq, tk), 0)
    nt = (((1,), (1,)), ((), ()))

    def body(j, carry):
        start = pl.multiple_of(j * tk, tk)
        ks = kb_ref[pl.ds(start, tk), :]
        vs = vb_ref[pl.ds(start, tk), :]
        mask = (start + lax.broadcasted_iota(jnp.int32, (tq, tk), 1)) <= row
        new = []
        for c in range(2):
            m, l, acc = carry[c]
            s = lax.dot_general(qs[c], ks[:, c * HEAD_DIM:(c + 1) * HEAD_DIM], nt,
                                preferred_element_type=F32) * scale
            s = jnp.where(mask, s, neg)
            m_new = jnp.maximum(m, jnp.max(s, axis=-1, keepdims=True))
            corr = jnp.exp(m - m_new)
            p = jnp.exp(s - m_new)
            l_new = l * corr + jnp.sum(p, axis=-1, keepdims=True)
            acc_new = acc * corr + jnp.dot(p.astype(BF16), vs, preferred_element_type=F32)
            new.append((m_new, l_new, acc_new))
        return tuple(new)

    init = tuple((jnp.full((tq, 1), neg, F32), jnp.zeros((tq, 1), F32),
                  jnp.zeros((tq, 2 * HEAD_DIM), F32)) for _ in range(2))
    n_kv = (i * tq + tq + tk - 1) // tk
    (m1, l1, a1), (m2, l2, a2) = lax.fori_loop(0, n_kv, body, init)
    o = a1 / l1 - lam * (a2 / l2)
    o_ref[...] = _subln(o, g_ref, lam_init).astype(o_ref.dtype)


def _prompt_attn(q, k, v, att_lambda, subln_g, layer, lam_init):
    b, seq, hv = q.shape
    hw = 2 * HEAD_DIM
    n_heads = hv // hw
    tq = _tile(seq, 256, SUBLANE)
    tk = tq
    body = functools.partial(_attn_body, tq=tq, tk=tk, lam_init=lam_init)
    kv_spec = pl.BlockSpec((None, seq, hw), lambda bb, h, i: (bb, 0, h))
    q_spec = pl.BlockSpec((None, tq, hw), lambda bb, h, i: (bb, i, h))
    return pl.pallas_call(
        body,
        grid=(b, n_heads, seq // tq),
        in_specs=[q_spec, kv_spec, kv_spec,
                  pl.BlockSpec((None, 4, HEAD_DIM), lambda bb, h, i: (layer, 0, 0)),
                  pl.BlockSpec((None, 1, hw), lambda bb, h, i: (layer, 0, 0))],
        out_specs=q_spec,
        out_shape=jax.ShapeDtypeStruct((b, seq, hv), BF16),
        scratch_shapes=[pltpu.VMEM((seq, hw), BF16), pltpu.VMEM((seq, hw), BF16)],
        compiler_params=_params("arbitrary", "arbitrary", "arbitrary"),
        name="prompt_attn",
    )(q, k, v, att_lambda, subln_g)


def _sample_attn_body(pt_ref, q_ref, kn_ref, vn_ref, ck_ref, cv_ref, lam_ref, g_ref, o_ref,
                      m_ref, l_ref, acc_ref, *, lam_init):
    p = pl.program_id(1)
    scale = HEAD_DIM ** -0.5
    q = q_ref[...]

    @pl.when(p == 0)
    def _():
        m_ref[...] = jnp.full(m_ref.shape, jnp.finfo(F32).min, F32)
        l_ref[...] = jnp.zeros(l_ref.shape, F32)
        acc_ref[...] = jnp.zeros(acc_ref.shape, F32)

    def scores(kk):
        prod = kk * q
        return [jnp.sum(prod[..., c * HEAD_DIM:(c + 1) * HEAD_DIM], axis=-1, keepdims=True) * scale
                for c in range(2)]

    kk = ck_ref[...]
    vv = cv_ref[...]
    for c, s in enumerate(scores(kk)):
        m = m_ref[c]
        m_new = jnp.maximum(m, jnp.max(s, axis=0))
        corr = jnp.exp(m - m_new)
        pr = jnp.exp(s - m_new)
        l_ref[c] = l_ref[c] * corr + jnp.sum(pr, axis=0)
        acc_ref[c] = acc_ref[c] * corr + jnp.sum(pr * vv, axis=0)
        m_ref[c] = m_new

    @pl.when(p == pl.num_programs(1) - 1)
    def _():
        lam = _lambda_value(lam_ref, lam_init)
        outs = []
        for c, s in enumerate(scores(kn_ref[...])):
            m = m_ref[c]
            m_new = jnp.maximum(m, s)
            corr = jnp.exp(m - m_new)
            pr = jnp.exp(s - m_new)
            l = l_ref[c] * corr + pr
            acc = acc_ref[c] * corr + pr * vn_ref[...]
            outs.append(acc / l)
        o = outs[0] - lam * outs[1]
        o_ref[...] = _subln(o, g_ref, lam_init).astype(o_ref.dtype)


def _sample_attn(q, k_new, v_new, cache_k, cache_v, page_table, att_lambda, subln_g, layer, lam_init):
    b, n_heads, hw = q.shape
    n_pages = page_table.shape[1]
    page = cache_k.shape[2]
    body = functools.partial(_sample_attn_body, lam_init=lam_init)
    row = pl.BlockSpec((None, n_heads, hw), lambda bb, p, pt: (bb, 0, 0))
    cache = pl.BlockSpec((None, None, page, n_heads, hw), lambda bb, p, pt: (layer, pt[bb, p], 0, 0, 0))
    grid_spec = pltpu.PrefetchScalarGridSpec(
        num_scalar_prefetch=1,
        grid=(b, n_pages),
        in_specs=[row, row, row, cache, cache,
                  pl.BlockSpec((None, 4, HEAD_DIM), lambda bb, p, pt: (layer, 0, 0)),
                  pl.BlockSpec((None, 1, hw), lambda bb, p, pt: (layer, 0, 0))],
        out_specs=row,
        scratch_shapes=[pltpu.VMEM((2, n_heads, 1), F32), pltpu.VMEM((2, n_heads, 1), F32),
                        pltpu.VMEM((2, n_heads, hw), F32)],
    )
    return pl.pallas_call(
        body,
        grid_spec=grid_spec,
        out_shape=jax.ShapeDtypeStruct((b, n_heads, hw), BF16),
        compiler_params=_params("arbitrary", "arbitrary"),
        name="sample_attn",
    )(page_table, q, k_new, v_new, cache_k, cache_v, att_lambda, subln_g)


def _mlp_norm(v, g_ref, b_ref):
    mu = jnp.mean(v, axis=-1, keepdims=True)
    vc = v - mu
    var = jnp.mean(vc * vc, axis=-1, keepdims=True)
    return vc * lax.rsqrt(var + LN_EPS) * g_ref[...] + b_ref[...]


def _spatial_body(u_ref, v_ref, g_ref, b_ref, ws_ref, bst_ref, o_ref):
    vn = _mlp_norm(v_ref[...], g_ref, b_ref).astype(BF16)
    t = ws_ref.shape[1]
    gc = vn.shape[1] // MLP_GROUPS
    causal = lax.broadcasted_iota(jnp.int32, (t, t), 1) <= lax.broadcasted_iota(jnp.int32, (t, t), 0)
    for g in range(MLP_GROUPS):
        cols = slice(g * gc, (g + 1) * gc)
        wm = jnp.where(causal, ws_ref[g], 0.0).astype(BF16)
        sp = jnp.dot(wm, vn[:, cols], preferred_element_type=F32) + bst_ref[:, g:g + 1]
        o_ref[:, cols] = (u_ref[:, cols] * sp).astype(o_ref.dtype)


def _spatial(uv, mlp_ln_g, mlp_ln_b, mlp_w_s, mlp_b_st, layer):
    r, c2 = uv.shape
    c = c2 // 2
    t = mlp_w_s.shape[2]
    return pl.pallas_call(
        _spatial_body,
        grid=(r // t,),
        in_specs=[pl.BlockSpec((t, c), lambda i: (i, 0)),
                  pl.BlockSpec((t, c), lambda i: (i, 1)),
                  pl.BlockSpec((None, 1, c), lambda i: (layer, 0, 0)),
                  pl.BlockSpec((None, 1, c), lambda i: (layer, 0, 0)),
                  pl.BlockSpec((None, MLP_GROUPS, t, t), lambda i: (layer, 0, 0, 0)),
                  pl.BlockSpec((None, t, MLP_GROUPS), lambda i: (layer, 0, 0))],
        out_specs=pl.BlockSpec((t, c), lambda i: (i, 0)),
        out_shape=jax.ShapeDtypeStruct((r, c), BF16),
        compiler_params=_params("arbitrary"),
        name="spatial",
    )(uv, uv, mlp_ln_g, mlp_ln_b, mlp_w_s, mlp_b_st)


def _spatial_first_body(u_ref, v_ref, g_ref, b_ref, ws_ref, bst_ref, o_ref, vn_ref):
    vn = _mlp_norm(v_ref[...], g_ref, b_ref)
    vn_ref[...] = vn
    gc = vn.shape[1] // MLP_GROUPS
    for g in range(MLP_GROUPS):
        cols = slice(g * gc, (g + 1) * gc)
        sp = ws_ref[g, 0:1, 0:1] * vn[:, cols] + bst_ref[0:1, g:g + 1]
        o_ref[:, cols] = (u_ref[:, cols] * sp).astype(o_ref.dtype)


def _spatial_first(uv, mlp_ln_g, mlp_ln_b, mlp_w_s, mlp_b_st, layer):
    r, c2 = uv.shape
    c = c2 // 2
    t = mlp_w_s.shape[2]
    return pl.pallas_call(
        _spatial_first_body,
        grid=(1,),
        in_specs=[pl.BlockSpec((r, c), lambda i: (0, 0)),
                  pl.BlockSpec((r, c), lambda i: (0, 1)),
                  pl.BlockSpec((None, 1, c), lambda i: (layer, 0, 0)),
                  pl.BlockSpec((None, 1, c), lambda i: (layer, 0, 0)),
                  pl.BlockSpec((None, MLP_GROUPS, t, t), lambda i: (layer, 0, 0, 0)),
                  pl.BlockSpec((None, t, MLP_GROUPS), lambda i: (layer, 0, 0))],
        out_specs=[pl.BlockSpec((r, c), lambda i: (0, 0)), pl.BlockSpec((r, c), lambda i: (0, 0))],
        out_shape=[jax.ShapeDtypeStruct((r, c), BF16), jax.ShapeDtypeStruct((r, c), F32)],
        compiler_params=_params("arbitrary"),
    )(uv, uv, mlp_ln_g, mlp_ln_b, mlp_w_s, mlp_b_st)


def _conv_body(cb_ref, cc_ref, ch_ref, w_ref, o_ref, st_ref):
    z = cc_ref[...] * ch_ref[...]
    seq = z.shape[0]
    row = lax.broadcasted_iota(jnp.int32, z.shape, 0)
    z1 = jnp.where(row >= 1, pltpu.roll(z, 1, 0), 0.0)
    z2 = jnp.where(row >= 2, pltpu.roll(z, 2, 0), 0.0)
    y = w_ref[0:1, :] * z2 + w_ref[1:2, :] * z1 + w_ref[2:3, :] * z
    o_ref[...] = (cb_ref[...] * y).astype(o_ref.dtype)
    st_ref[...] = z[seq - (CONV_W - 1):, :]


def _conv_prompt(cz, conv_w, layer, batch):
    r, c3 = cz.shape
    c = c3 // 3
    seq = r // batch
    tc = _tile(c, 256)
    nb = c // tc
    cz3 = cz.reshape(batch, seq, c3)
    blk = lambda off: pl.BlockSpec((None, seq, tc), lambda b, j: (b, 0, off + j))
    y, st = pl.pallas_call(
        _conv_body,
        grid=(batch, nb),
        in_specs=[blk(0), blk(nb), blk(2 * nb),
                  pl.BlockSpec((None, CONV_W, tc), lambda b, j: (layer, 0, j))],
        out_specs=[pl.BlockSpec((None, seq, tc), lambda b, j: (b, 0, j)),
                   pl.BlockSpec((None, CONV_W - 1, tc), lambda b, j: (b, 0, j))],
        out_shape=[jax.ShapeDtypeStruct((batch, seq, c), BF16),
                   jax.ShapeDtypeStruct((batch, CONV_W - 1, c), F32)],
        compiler_params=_params("arbitrary", "arbitrary"),
        name="conv",
    )(cz3, cz3, cz3, conv_w)
    return y.reshape(r, c), st


def _conv_step_body(cb_ref, cc_ref, ch_ref, prev_ref, w_ref, o_ref, st_ref):
    z = cc_ref[...] * ch_ref[...]
    c = z.shape[1]
    p0 = prev_ref[:, :c]
    p1 = prev_ref[:, c:]
    y = w_ref[0:1, :] * p0 + w_ref[1:2, :] * p1 + w_ref[2:3, :] * z
    o_ref[...] = (cb_ref[...] * y).astype(o_ref.dtype)
    st_ref[:, :c] = p1
    st_ref[:, c:] = z


def _conv_step(cz, state_flat, conv_w, layer):
    r, c3 = cz.shape
    c = c3 // 3
    blk = lambda j: pl.BlockSpec((r, c), lambda i: (0, j))
    y, st = pl.pallas_call(
        _conv_step_body,
        grid=(1,),
        in_specs=[blk(0), blk(1), blk(2),
                  pl.BlockSpec((None, r, (CONV_W - 1) * c), lambda i: (layer, 0, 0)),
                  pl.BlockSpec((None, CONV_W, c), lambda i: (layer, 0, 0))],
        out_specs=[pl.BlockSpec((r, c), lambda i: (0, 0)),
                   pl.BlockSpec((r, (CONV_W - 1) * c), lambda i: (0, 0))],
        out_shape=[jax.ShapeDtypeStruct((r, c), BF16),
                   jax.ShapeDtypeStruct((r, (CONV_W - 1) * c), F32)],
        compiler_params=_params("arbitrary"),
    )(cz, cz, cz, state_flat, conv_w)
    return y, st.reshape(r, CONV_W - 1, c)


def _branch_body(ya_ref, ym_ref, yc_ref, w_ref, ga_ref, gm_ref, gc_ref, o_ref, wb_ref):
    @pl.when(pl.program_id(1) == 0)
    def _():
        for t in range(N_BRANCH):
            k = w_ref.shape[1]
            _cast_weight(w_ref.at[t], wb_ref, t, _tile(k, 256, SUBLANE))

    total = None
    for t, (y_ref, g_ref) in enumerate(((ya_ref, ga_ref), (ym_ref, gm_ref), (yc_ref, gc_ref))):
        term = g_ref[...] * jnp.dot(y_ref[...], wb_ref[t], preferred_element_type=F32)
        total = term if total is None else total + term
    o_ref[...] = total.astype(o_ref.dtype)


def _branch_merge(y_att, y_mlp, y_conv, gates, w_branch, layer):
    r, k = y_att.shape
    d = w_branch.shape[3]
    tn = _tile(d, 512)
    tm = _tile(r, 512, SUBLANE)
    nb = d // tn
    y_spec = pl.BlockSpec((tm, k), lambda j, i: (i, 0))
    g_spec = lambda t: pl.BlockSpec((tm, tn), lambda j, i: (i, t * nb + j))
    return pl.pallas_call(
        _branch_body,
        grid=(nb, r // tm),
        in_specs=[y_spec, y_spec, y_spec,
                  pl.BlockSpec((None, N_BRANCH, k, tn), lambda j, i: (layer, 0, 0, j)),
                  g_spec(0), g_spec(1), g_spec(2)],
        out_specs=pl.BlockSpec((tm, tn), lambda j, i: (i, j)),
        out_shape=jax.ShapeDtypeStruct((r, d), BF16),
        scratch_shapes=[pltpu.VMEM((N_BRANCH, k, tn), BF16)],
        compiler_params=_params("arbitrary", "arbitrary"),
        name="branch_merge",
    )(y_att, y_mlp, y_conv, w_branch, gates, gates, gates)


def _rope_tables(pos):
    half = HEAD_DIM // 2
    inv = ROPE_THETA ** (-jnp.arange(half, dtype=F32) * 2.0 / HEAD_DIM)
    ang = pos.astype(F32)[:, None] * inv[None, :]
    cos, sin = jnp.cos(ang), jnp.sin(ang)
    return jnp.concatenate([cos, cos], axis=-1), jnp.concatenate([-sin, sin], axis=-1)


def _layer(layer, x, xb, p, rope, seq, w, attend, spatial, conv):
    r, d = x.shape
    depth = w["ln_g"].shape[0]
    alpha = (2 * depth) ** 0.25
    att = w["n_heads"] * 2 * HEAD_DIM
    mlp_ch = w["mlp_ln_g"].shape[2]
    conv_ch = w["conv_w"].shape[2]

    y = _ffn_out(_ffn_in(xb, w["ffn1_w_in"], layer), w["ffn1_w_out"], layer)
    x, xb = _res_ln(x, y, w["ln_g"], w["ln_b"], layer, 0, alpha, 0.5)

    cos_t, sin_t = rope
    tm = _row_tile(cos_t.shape[0])
    n_seq_blocks = cos_t.shape[0] // tm
    rope_specs = [pl.BlockSpec((tm, HEAD_DIM), lambda j, i: (i % n_seq_blocks, 0))] * 2
    wmi = w["w_mix_in"]
    off = 0
    q = _proj(xb, wmi, layer, off, att, _epi_rope, F32, (cos_t, sin_t), rope_specs, tm=tm); off += att
    k = _proj(xb, wmi, layer, off, att, _epi_rope, F32, (cos_t, sin_t), rope_specs, tm=tm); off += att
    v = _proj(xb, wmi, layer, off, att, _epi_plain, F32); off += att
    uv = _proj(xb, wmi, layer, off, 2 * mlp_ch, _epi_gelu, F32); off += 2 * mlp_ch
    cz = _proj(xb, wmi, layer, off, 3 * conv_ch, _epi_plain, F32); off += 3 * conv_ch
    gates = _proj(xb, wmi, layer, off, N_BRANCH * d, _epi_sigmoid, F32)

    lam_init = 0.8 - 0.6 * math.exp(-0.3 * layer)
    y_att = attend(layer, q, k, v, lam_init)
    y_mlp, vn = spatial(layer, uv)
    y_conv, conv_state = conv(layer, cz)

    mixed = _branch_merge(y_att, y_mlp, y_conv, gates, w["w_branch"], layer)
    y = _proj(mixed, w["w_mix_out"], layer, 0, d, _epi_plain, F32)
    x, xb = _res_ln(x, y, w["ln_g"], w["ln_b"], layer, 1, alpha, 1.0)

    y = _ffn_out(_ffn_in(xb, w["ffn2_w_in"], layer), w["ffn2_w_out"], layer)
    x, xb = _res_ln(x, y, w["ln_g"], w["ln_b"], layer, 2, alpha, 0.5)

    x, xb = _ple(xb, x, p, w["ple_w_gate"], w["ple_w_proj"], layer)
    return x, xb, k, v, conv_state, vn


def _cast_body(x_ref, o_ref):
    o_ref[...] = x_ref[...].astype(o_ref.dtype)


def _to_bf16(x):
    r, d = x.shape
    tm = _tile(r, 512, SUBLANE)
    spec = pl.BlockSpec((tm, d), lambda i: (i, 0))
    return pl.pallas_call(
        _cast_body, grid=(r // tm,), in_specs=[spec], out_specs=spec,
        out_shape=jax.ShapeDtypeStruct((r, d), BF16), compiler_params=_params("arbitrary"),
    )(x)


def kernel(x_prompt, x_sample, cache_k, cache_v, state_conv, page_table, p_prompt, p_sample,
           ln_g, ln_b, ffn1_w_in, ffn1_w_out, w_mix_in, att_lambda, att_subln_g, mlp_ln_g, mlp_ln_b,
           mlp_w_s, mlp_b_s, conv_w, w_branch, w_mix_out, ffn2_w_in, ffn2_w_out, ple_w_gate, ple_w_proj):
    bp, lp, d = x_prompt.shape
    bs, ls, _ = x_sample.shape
    assert ls == 1, "the sample group advances one position per step"
    depth = ln_g.shape[0]
    n_heads = cache_k.shape[3]
    hw = 2 * HEAD_DIM
    past = page_table.shape[1] * PAGE_SIZE
    rp, rs = bp * lp, bs * ls

    w = dict(
        ln_g=ln_g, ln_b=ln_b, ffn1_w_in=ffn1_w_in, ffn1_w_out=ffn1_w_out, w_mix_in=w_mix_in,
        att_subln_g=att_subln_g.reshape(depth, 1, hw), mlp_ln_g=mlp_ln_g.reshape(depth, 1, -1),
        mlp_ln_b=mlp_ln_b.reshape(depth, 1, -1), conv_w=conv_w, w_branch=w_branch, w_mix_out=w_mix_out,
        ffn2_w_in=ffn2_w_in, ffn2_w_out=ffn2_w_out, ple_w_gate=ple_w_gate, ple_w_proj=ple_w_proj,
        n_heads=n_heads)
    mlp_b_st = jnp.swapaxes(mlp_b_s, 1, 2)
    state_flat = state_conv.reshape(depth, bs, -1)
    subln_g = w["att_subln_g"]

    rope_p = _rope_tables(jnp.arange(lp))
    rope_s = tuple(jnp.broadcast_to(t, (rs, HEAD_DIM)) for t in _rope_tables(past + jnp.arange(ls)))

    def attend_p(layer, q, k, v, lam_init):
        shp = (bp, lp, n_heads * hw)
        y = _prompt_attn(q.reshape(shp), k.reshape(shp), v.reshape(shp), att_lambda, subln_g, layer, lam_init)
        return y.reshape(rp, n_heads * hw)

    def attend_s(layer, q, k, v, lam_init):
        shp = (rs, n_heads, hw)
        y = _sample_attn(q.reshape(shp), k.reshape(shp), v.reshape(shp), cache_k, cache_v, page_table,
                         att_lambda, subln_g, layer, lam_init)
        return y.reshape(rs, n_heads * hw)

    def spatial_p(layer, uv):
        return _spatial(uv, w["mlp_ln_g"], w["mlp_ln_b"], mlp_w_s, mlp_b_st, layer), None

    def spatial_s(layer, uv):
        return _spatial_first(uv, w["mlp_ln_g"], w["mlp_ln_b"], mlp_w_s, mlp_b_st, layer)

    def conv_p(layer, cz):
        return _conv_prompt(cz, conv_w, layer, bp)

    def conv_s(layer, cz):
        return _conv_step(cz, state_flat, conv_w, layer)

    xp = x_prompt.reshape(rp, d)
    xs = x_sample.reshape(rs, d)
    xpb, xsb = _to_bf16(xp), _to_bf16(xs)
    pp = p_prompt.reshape(depth, rp, -1)
    ps = p_sample.reshape(depth, rs, -1)

    kp_l, vp_l, cp_l, ks_l, vs_l, cs_l, chs_l = [], [], [], [], [], [], []
    for layer in range(depth):
        xp, xpb, kp, vp, cp, _ = _layer(layer, xp, xpb, pp, rope_p, lp, w, attend_p, spatial_p, conv_p)
        xs, xsb, k_s, v_s, c_s, vn_s = _layer(layer, xs, xsb, ps, rope_s, ls, w, attend_s, spatial_s, conv_s)
        kp_l.append(kp.reshape(bp, lp, n_heads, hw)); vp_l.append(vp.reshape(bp, lp, n_heads, hw)); cp_l.append(cp)
        ks_l.append(k_s.reshape(bs, ls, n_heads, hw)); vs_l.append(v_s.reshape(bs, ls, n_heads, hw))
        cs_l.append(c_s); chs_l.append(vn_s.reshape(bs, ls, -1))

    return (xp.reshape(bp, lp, d), xs.reshape(bs, ls, d), jnp.stack(kp_l), jnp.stack(vp_l), jnp.stack(cp_l),
            jnp.stack(ks_l), jnp.stack(vs_l), jnp.stack(cs_l), jnp.stack(chs_l))
```

```python
import functools
import math

import jax
import jax.numpy as jnp
from jax import lax
from jax.experimental import pallas as pl
from jax.experimental.pallas import tpu as pltpu

F32 = jnp.float32
BF16 = jnp.bfloat16

HEAD_DIM = 128
PAGE_SIZE = 128
MLP_GROUPS = 8
CONV_W = 3
N_BRANCH = 3
LN_EPS = 1e-5
ROPE_THETA = 10000.0

VMEM_LIMIT_BYTES = 56 * 1024 * 1024
LANE = 128
SUBLANE = 8

SOFTMAX_EXP2_SCALE = HEAD_DIM ** -0.5 * math.log2(math.e)
NEG = float(jnp.finfo(jnp.float32).min)
NT_DIMS = (((1,), (1,)), ((), ()))


def _tile(n, pref, mult=LANE):
    t = (min(pref, n) // mult) * mult
    while t >= mult:
        if n % t == 0:
            return t
        t -= mult
    return n


def _params(*sem):
    return pltpu.CompilerParams(dimension_semantics=sem, vmem_limit_bytes=VMEM_LIMIT_BYTES)


def _sigmoid(x):
    return 1.0 / (1.0 + jnp.exp(-x))


def _gelu_tanh(x):
    c = math.sqrt(2.0 / math.pi)
    return 0.5 * x * (1.0 + jnp.tanh(c * (x + 0.044715 * (x * x * x))))


def _cast_weight(w_ref, wb_ref, t, rows_per_step):
    k = w_ref.shape[0]
    steps = k // rows_per_step

    def body(r, carry):
        rows = pl.ds(pl.multiple_of(r * rows_per_step, rows_per_step), rows_per_step)
        wb_ref[t, rows, :] = w_ref[rows, :].astype(BF16)
        return carry

    lax.fori_loop(0, steps, body, 0)


def _mm_body(*refs, n_w, n_xp, n_xs, n_sh, n_o, epi):
    it = iter(refs)
    a_ref, as_ref = next(it), next(it)
    w_refs = [next(it) for _ in range(n_w)]
    xp_refs = [next(it) for _ in range(n_xp)]
    xs_refs = [next(it) for _ in range(n_xs)]
    sh_refs = [next(it) for _ in range(n_sh)]
    op_refs = [next(it) for _ in range(n_o)]
    os_refs = [next(it) for _ in range(n_o)]
    wb_ref = next(it)

    @pl.when(pl.program_id(1) == 0)
    def _():
        for t in range(n_w):
            k = w_refs[t].shape[0]
            _cast_weight(w_refs[t], wb_ref, t, _tile(k, 256, SUBLANE))
        a_s = as_ref[...]
        accs = [jnp.dot(a_s, wb_ref[t], preferred_element_type=F32) for t in range(n_w)]
        epi(accs, xs_refs + sh_refs, os_refs)

    a = a_ref[...]
    accs = [jnp.dot(a, wb_ref[t], preferred_element_type=F32) for t in range(n_w)]
    epi(accs, xp_refs + sh_refs, op_refs)


def _mm_call(a, a_s, ws, w_specs, xp, xp_specs, xs, xs_specs, sh, sh_specs, out_dtypes, n_cols, epi, *, tm, tn):
    r, k = a.shape
    rs = a_s.shape[0]
    n_o = len(out_dtypes)
    body = functools.partial(_mm_body, n_w=len(ws), n_xp=len(xp), n_xs=len(xs), n_sh=len(sh), n_o=n_o, epi=epi)
    outs = pl.pallas_call(
        body,
        grid=(n_cols // tn, r // tm),
        in_specs=[pl.BlockSpec((tm, k), lambda j, i: (i, 0)),
                  pl.BlockSpec((rs, k), lambda j, i: (0, 0)),
                  *w_specs, *xp_specs, *xs_specs, *sh_specs],
        out_specs=[pl.BlockSpec((tm, tn), lambda j, i: (i, j))] * n_o
                  + [pl.BlockSpec((rs, tn), lambda j, i: (0, j))] * n_o,
        out_shape=[jax.ShapeDtypeStruct((r, n_cols), dt) for dt in out_dtypes]
                  + [jax.ShapeDtypeStruct((rs, n_cols), dt) for dt in out_dtypes],
        scratch_shapes=[pltpu.VMEM((len(ws), k, tn), BF16)],
        compiler_params=_params("arbitrary", "arbitrary"),
        name="mm" + epi.__name__[len("_epi"):],
    )(a, a_s, *ws, *xp, *xs, *sh)
    return outs[:n_o], outs[n_o:]


def _row_tile(r):
    return _tile(r, 1024, SUBLANE)


def _w_spec(layer, k, tn, off_blocks):
    return pl.BlockSpec((None, k, tn), lambda j, i: (layer, 0, off_blocks + j))


def _epi_swiglu(accs, x_refs, o_refs):
    a, b = accs
    o_refs[0][...] = (a * _sigmoid(a) * b).astype(o_refs[0].dtype)


def _epi_plain(accs, x_refs, o_refs):
    o_refs[0][...] = accs[0].astype(o_refs[0].dtype)


def _epi_gelu(accs, x_refs, o_refs):
    o_refs[0][...] = _gelu_tanh(accs[0]).astype(o_refs[0].dtype)


def _epi_sigmoid(accs, x_refs, o_refs):
    o_refs[0][...] = _sigmoid(accs[0]).astype(o_refs[0].dtype)


def _epi_rope(accs, x_refs, o_refs):
    cos = x_refs[0][...]
    sin = x_refs[1][...]
    acc = accs[0]
    for c in range(acc.shape[1] // HEAD_DIM):
        cols = slice(c * HEAD_DIM, (c + 1) * HEAD_DIM)
        x = acc[:, cols]
        o_refs[0][:, cols] = (x * cos + pltpu.roll(x, HEAD_DIM // 2, 1) * sin).astype(o_refs[0].dtype)


def _epi_ple(accs, x_refs, o_refs):
    x_ref, p_ref, pp_ref = x_refs
    proj = jnp.dot(p_ref[...].astype(BF16), pp_ref[...].astype(BF16), preferred_element_type=F32)
    out = x_ref[...] + _sigmoid(accs[0]) * proj
    o_refs[0][...] = out
    o_refs[1][...] = out.astype(BF16)


def _ffn_in(xb, w_in, layer):
    xp, xs = xb
    d = xp.shape[1]
    f = w_in.shape[2] // 2
    tn = _tile(f, 256)
    nb = f // tn
    (hp,), (hs,) = _mm_call(
        xp, xs, [w_in, w_in], [_w_spec(layer, d, tn, 0), _w_spec(layer, d, tn, nb)],
        [], [], [], [], [], [], [BF16], f, _epi_swiglu, tm=_row_tile(xp.shape[0]), tn=tn)
    return hp, hs


def _proj(xb, w, layer, col_off, n_cols, epi, out_dtype, xp=(), xp_specs=(), xs=(), xs_specs=(), tm=None):
    a, a_s = xb
    k = a.shape[1]
    tn = _tile(math.gcd(n_cols, col_off), 512)
    assert col_off % tn == 0 and n_cols % tn == 0
    tm = _row_tile(a.shape[0]) if tm is None else tm
    (op,), (os_,) = _mm_call(
        a, a_s, [w], [_w_spec(layer, k, tn, col_off // tn)],
        list(xp), list(xp_specs), list(xs), list(xs_specs), [], [],
        [out_dtype], n_cols, epi, tm=tm, tn=tn)
    return op, os_


def _ple(xb, x, p, w_gate, w_proj, layer):
    (ap, as_), (xp, xs), (pp, ps) = xb, x, p
    r, d = ap.shape
    rs = as_.shape[0]
    e = pp.shape[-1]
    tn = _tile(d, 512)
    tm = _tile(r, 512, SUBLANE)
    (of, ob), (sf, sb) = _mm_call(
        ap, as_, [w_gate], [_w_spec(layer, d, tn, 0)],
        [xp, pp], [pl.BlockSpec((tm, tn), lambda j, i: (i, j)),
                   pl.BlockSpec((None, tm, e), lambda j, i: (layer, i, 0))],
        [xs, ps], [pl.BlockSpec((rs, tn), lambda j, i: (0, j)),
                   pl.BlockSpec((None, rs, e), lambda j, i: (layer, 0, 0))],
        [w_proj], [pl.BlockSpec((None, e, tn), lambda j, i: (layer, 0, j))],
        [F32, BF16], d, _epi_ple, tm=tm, tn=tn)
    return (of, sf), (ob, sb)


def _ffn_out_body(h_ref, hs_ref, w_ref, o_ref, os_ref):
    first_rows = pl.program_id(0) == 0

    @pl.when(pl.program_id(1) == 0)
    def _():
        o_ref[...] = jnp.zeros(o_ref.shape, F32)

    @pl.when(first_rows & (pl.program_id(1) == 0))
    def _():
        os_ref[...] = jnp.zeros(os_ref.shape, F32)

    h = h_ref[...]
    d = o_ref.shape[1]
    tc = _tile(d, 1024)
    for c in range(d // tc):
        cols = slice(c * tc, (c + 1) * tc)
        o_ref[:, cols] += jnp.dot(h, w_ref[:, cols].astype(BF16), preferred_element_type=F32)

    @pl.when(first_rows)
    def _():
        os_ref[...] += jnp.dot(hs_ref[...], w_ref[...].astype(BF16), preferred_element_type=F32)


def _ffn_out(h, w_out, layer):
    hp, hs = h
    r, f = hp.shape
    rs = hs.shape[0]
    d = w_out.shape[2]
    tk = _tile(f, 256)
    tm = _row_tile(r)
    return pl.pallas_call(
        _ffn_out_body,
        grid=(r // tm, f // tk),
        in_specs=[pl.BlockSpec((tm, tk), lambda i, k: (i, k)),
                  pl.BlockSpec((rs, tk), lambda i, k: (0, k)),
                  pl.BlockSpec((None, tk, d), lambda i, k: (layer, k, 0))],
        out_specs=[pl.BlockSpec((tm, d), lambda i, k: (i, 0)),
                   pl.BlockSpec((rs, d), lambda i, k: (0, 0))],
        out_shape=[jax.ShapeDtypeStruct((r, d), F32), jax.ShapeDtypeStruct((rs, d), F32)],
        compiler_params=_params("arbitrary", "arbitrary"),
        name="ffn_out",
    )(hp, hs, w_out)


def _ln_body(x_ref, y_ref, g_ref, b_ref, of_ref, ob_ref, *, alpha, s, which):
    z = alpha * x_ref[...] + s * y_ref[...]
    mu = jnp.mean(z, axis=-1, keepdims=True)
    zc = z - mu
    var = jnp.mean(zc * zc, axis=-1, keepdims=True)
    out = zc * lax.rsqrt(var + LN_EPS) * g_ref[which:which + 1, :] + b_ref[which:which + 1, :]
    of_ref[...] = out
    ob_ref[...] = out.astype(BF16)


def _res_ln_one(x, y, ln_g, ln_b, layer, which, alpha, s):
    r, d = x.shape
    tm = _tile(r, 256, SUBLANE)
    n_ln = ln_g.shape[1]
    body = functools.partial(_ln_body, alpha=alpha, s=s, which=which)
    row = pl.BlockSpec((tm, d), lambda i: (i, 0))
    par = pl.BlockSpec((None, n_ln, d), lambda i: (layer, 0, 0))
    return pl.pallas_call(
        body,
        grid=(r // tm,),
        in_specs=[row, row, par, par],
        out_specs=[row, row],
        out_shape=[jax.ShapeDtypeStruct((r, d), F32), jax.ShapeDtypeStruct((r, d), BF16)],
        compiler_params=_params("arbitrary"),
        name="res_ln",
    )(x, y, ln_g, ln_b)


def _res_ln(x, y, ln_g, ln_b, layer, which, alpha, s):
    pf, pb = _res_ln_one(x[0], y[0], ln_g, ln_b, layer, which, alpha, s)
    sf, sb = _res_ln_one(x[1], y[1], ln_g, ln_b, layer, which, alpha, s)
    return (pf, sf), (pb, sb)


def _lambda_value(lam_ref, lam_init):
    lf = lam_ref[...]
    a = jnp.sum(lf[0:1, :] * lf[1:2, :], axis=-1, keepdims=True)
    b = jnp.sum(lf[2:3, :] * lf[3:4, :], axis=-1, keepdims=True)
    return jnp.exp(a) - jnp.exp(b) + lam_init


def _attn_body(q_ref, k_ref, v_ref, lam_ref, gt_ref, o_ref, kb_ref, vt_ref, *, tq, lam_init):
    i = pl.program_id(2)
    n_blocks = k_ref.shape[0] // tq

    @pl.when(i == 0)
    def _():
        for r in range(n_blocks):
            rows = slice(r * tq, (r + 1) * tq)
            kb_ref[rows, :] = k_ref[rows, :].astype(BF16)
            vt_ref[:, rows] = v_ref[rows, :].T.astype(BF16)

    q = q_ref[...]
    qs = (q[:, :HEAD_DIM].astype(BF16), q[:, HEAD_DIM:].astype(BF16))
    lam = _lambda_value(lam_ref, lam_init)
    keep = lax.broadcasted_iota(jnp.int32, (tq, tq), 0) <= lax.broadcasted_iota(jnp.int32, (tq, tq), 1)

    def scores(c, rows):
        return lax.dot_general(kb_ref[rows, c * HEAD_DIM:(c + 1) * HEAD_DIM], qs[c], NT_DIMS,
                               preferred_element_type=F32)

    def probs(s, m):
        p = jnp.exp2((s - m) * SOFTMAX_EXP2_SCALE)
        return p.astype(BF16), jnp.sum(p, axis=0, keepdims=True)

    def attend(n_full):
        full = slice(0, n_full * tq)
        diag = slice(n_full * tq, (n_full + 1) * tq)
        outs = []
        for c in range(2):
            s_d = jnp.where(keep, scores(c, diag), NEG)
            m = jnp.max(s_d, axis=0, keepdims=True)
            if n_full:
                s_f = scores(c, full)
                m = jnp.maximum(m, jnp.max(s_f, axis=0, keepdims=True))
            p_d, l = probs(s_d, m)
            acc = jnp.dot(vt_ref[:, diag], p_d, preferred_element_type=F32)
            if n_full:
                p_f, l_f = probs(s_f, m)
                l = l + l_f
                acc = acc + jnp.dot(vt_ref[:, full], p_f, preferred_element_type=F32)
            outs.append(acc * (1.0 / l))
        ot = outs[0] - lam * outs[1]
        ms = jnp.mean(ot * ot, axis=0, keepdims=True)
        ot = ot * lax.rsqrt(ms + LN_EPS) * gt_ref[...] * (1.0 - lam_init)
        o_ref[...] = ot.T.astype(o_ref.dtype)

    for n_full in range(n_blocks):
        pl.when(i == n_full)(functools.partial(attend, n_full))


def _prompt_attn(q, k, v, att_lambda, subln_gt, layer, lam_init):
    b, seq, hv = q.shape
    hw = 2 * HEAD_DIM
    n_heads = hv // hw
    tq = _tile(seq, 256, LANE)
    body = functools.partial(_attn_body, tq=tq, lam_init=lam_init)
    kv_spec = pl.BlockSpec((None, seq, hw), lambda bb, h, i: (bb, 0, h))
    q_spec = pl.BlockSpec((None, tq, hw), lambda bb, h, i: (bb, i, h))
    return pl.pallas_call(
        body,
        grid=(b, n_heads, seq // tq),
        in_specs=[q_spec, kv_spec, kv_spec,
                  pl.BlockSpec((None, 4, HEAD_DIM), lambda bb, h, i: (layer, 0, 0)),
                  pl.BlockSpec((None, hw, 1), lambda bb, h, i: (layer, 0, 0))],
        out_specs=q_spec,
        out_shape=jax.ShapeDtypeStruct((b, seq, hv), BF16),
        scratch_shapes=[pltpu.VMEM((seq, hw), BF16), pltpu.VMEM((hw, seq), BF16)],
        compiler_params=_params("arbitrary", "arbitrary", "arbitrary"),
        name="prompt_attn",
    )(q, k, v, att_lambda, subln_gt)


def _sample_attn_body(pt_ref, q_ref, kn_ref, vn_ref, *rest, n_pp, lam_init):
    ck_refs, cv_refs = rest[:n_pp], rest[n_pp:2 * n_pp]
    lam_ref, g_ref, o_ref, m_ref, l_ref, acc_ref = rest[2 * n_pp:]
    step_id = pl.program_id(1)
    n_heads, hw = q_ref.shape
    rows = ck_refs[0].shape[0] * n_heads

    q = q_ref[...]
    lane = lax.broadcasted_iota(jnp.int32, (n_heads, hw), 1)
    qm = jnp.concatenate([jnp.where(lane < HEAD_DIM, q, 0.0), jnp.where(lane >= HEAD_DIM, q, 0.0)], axis=0)
    qmb = qm.astype(BF16)

    @pl.when(step_id == 0)
    def _():
        m_ref[...] = jnp.full(m_ref.shape, NEG, F32)
        l_ref[...] = jnp.zeros(l_ref.shape, F32)
        acc_ref[...] = jnp.zeros(acc_ref.shape, F32)

    head_mask = n_heads - 1
    own_head = ((lax.broadcasted_iota(jnp.int32, (2 * n_heads, rows), 1) & head_mask)
                == (lax.broadcasted_iota(jnp.int32, (2 * n_heads, rows), 0) & head_mask))

    for ck_ref, cv_ref in zip(ck_refs, cv_refs):
        kk = ck_ref[...].reshape(rows, hw).astype(BF16)
        vv = cv_ref[...].reshape(rows, hw).astype(BF16)
        s = lax.dot_general(qmb, kk, NT_DIMS, preferred_element_type=F32)
        s = jnp.where(own_head, s, NEG)
        m = m_ref[...]
        m_new = jnp.maximum(m, jnp.max(s, axis=-1, keepdims=True))
        corr = jnp.exp2((m - m_new) * SOFTMAX_EXP2_SCALE)
        pr = jnp.exp2((s - m_new) * SOFTMAX_EXP2_SCALE)
        l_ref[...] = l_ref[...] * corr + jnp.sum(pr, axis=-1, keepdims=True)
        acc_ref[...] = acc_ref[...] * corr + jnp.dot(pr.astype(BF16), vv, preferred_element_type=F32)
        m_ref[...] = m_new

    @pl.when(step_id == pl.num_programs(1) - 1)
    def _():
        kn = kn_ref[...]
        vn = vn_ref[...]
        s = jnp.sum(qm * jnp.concatenate([kn, kn], axis=0), axis=-1, keepdims=True)
        m = m_ref[...]
        m_new = jnp.maximum(m, s)
        corr = jnp.exp2((m - m_new) * SOFTMAX_EXP2_SCALE)
        pr = jnp.exp2((s - m_new) * SOFTMAX_EXP2_SCALE)
        l = l_ref[...] * corr + pr
        acc = acc_ref[...] * corr + pr * jnp.concatenate([vn, vn], axis=0)
        out = acc * (1.0 / l)
        o = out[:n_heads] - _lambda_value(lam_ref, lam_init) * out[n_heads:]
        ms = jnp.mean(o * o, axis=-1, keepdims=True)
        o_ref[...] = (o * lax.rsqrt(ms + LN_EPS) * g_ref[...] * (1.0 - lam_init)).astype(o_ref.dtype)


def _sample_attn(q, k_new, v_new, cache_k, cache_v, page_table, att_lambda, subln_g, layer, lam_init):
    b, n_heads, hw = q.shape
    assert n_heads & (n_heads - 1) == 0
    n_pages = page_table.shape[1]
    page = cache_k.shape[2]
    n_pp = 4 if n_pages % 4 == 0 else (2 if n_pages % 2 == 0 else 1)
    body = functools.partial(_sample_attn_body, n_pp=n_pp, lam_init=lam_init)
    row = pl.BlockSpec((None, n_heads, hw), lambda bb, p, pt: (bb, 0, 0))

    def cache(t):
        return pl.BlockSpec((None, None, page, n_heads, hw),
                            lambda bb, p, pt: (layer, pt[bb, p * n_pp + t], 0, 0, 0))

    caches = [cache(t) for t in range(n_pp)]
    grid_spec = pltpu.PrefetchScalarGridSpec(
        num_scalar_prefetch=1,
        grid=(b, n_pages // n_pp),
        in_specs=[row, row, row, *caches, *caches,
                  pl.BlockSpec((None, 4, HEAD_DIM), lambda bb, p, pt: (layer, 0, 0)),
                  pl.BlockSpec((None, 1, hw), lambda bb, p, pt: (layer, 0, 0))],
        out_specs=row,
        scratch_shapes=[pltpu.VMEM((2 * n_heads, 1), F32), pltpu.VMEM((2 * n_heads, 1), F32),
                        pltpu.VMEM((2 * n_heads, hw), F32)],
    )
    return pl.pallas_call(
        body,
        grid_spec=grid_spec,
        out_shape=jax.ShapeDtypeStruct((b, n_heads, hw), BF16),
        compiler_params=_params("arbitrary", "arbitrary"),
        name="sample_attn",
    )(page_table, q, k_new, v_new, *([cache_k] * n_pp), *([cache_v] * n_pp), att_lambda, subln_g)


def _mlp_norm(v, g_ref, b_ref):
    mu = jnp.mean(v, axis=-1, keepdims=True)
    vc = v - mu
    var = jnp.mean(vc * vc, axis=-1, keepdims=True)
    return vc * lax.rsqrt(var + LN_EPS) * g_ref[...] + b_ref[...]


def _spatial_body(u_ref, v_ref, g_ref, b_ref, ws_ref, bst_ref, o_ref):
    vn = _mlp_norm(v_ref[...], g_ref, b_ref).astype(BF16)
    t = ws_ref.shape[1]
    gc = vn.shape[1] // MLP_GROUPS
    causal = lax.broadcasted_iota(jnp.int32, (t, t), 1) <= lax.broadcasted_iota(jnp.int32, (t, t), 0)
    for g in range(MLP_GROUPS):
        cols = slice(g * gc, (g + 1) * gc)
        wm = jnp.where(causal, ws_ref[g], 0.0).astype(BF16)
        sp = jnp.dot(wm, vn[:, cols], preferred_element_type=F32) + bst_ref[:, g:g + 1]
        o_ref[:, cols] = (u_ref[:, cols] * sp).astype(o_ref.dtype)


def _spatial(uv, mlp_ln_g, mlp_ln_b, mlp_w_s, mlp_b_st, layer):
    r, c2 = uv.shape
    c = c2 // 2
    t = mlp_w_s.shape[2]
    return pl.pallas_call(
        _spatial_body,
        grid=(r // t,),
        in_specs=[pl.BlockSpec((t, c), lambda i: (i, 0)),
                  pl.BlockSpec((t, c), lambda i: (i, 1)),
                  pl.BlockSpec((None, 1, c), lambda i: (layer, 0, 0)),
                  pl.BlockSpec((None, 1, c), lambda i: (layer, 0, 0)),
                  pl.BlockSpec((None, MLP_GROUPS, t, t), lambda i: (layer, 0, 0, 0)),
                  pl.BlockSpec((None, t, MLP_GROUPS), lambda i: (layer, 0, 0))],
        out_specs=pl.BlockSpec((t, c), lambda i: (i, 0)),
        out_shape=jax.ShapeDtypeStruct((r, c), BF16),
        compiler_params=_params("arbitrary"),
        name="spatial",
    )(uv, uv, mlp_ln_g, mlp_ln_b, mlp_w_s, mlp_b_st)


def _spatial_first_body(u_ref, v_ref, g_ref, b_ref, ws_ref, bst_ref, o_ref, vn_ref):
    vn = _mlp_norm(v_ref[...], g_ref, b_ref)
    vn_ref[...] = vn
    gc = vn.shape[1] // MLP_GROUPS
    for g in range(MLP_GROUPS):
        cols = slice(g * gc, (g + 1) * gc)
        sp = ws_ref[g, 0:1, 0:1] * vn[:, cols] + bst_ref[0:1, g:g + 1]
        o_ref[:, cols] = (u_ref[:, cols] * sp).astype(o_ref.dtype)


def _spatial_first(uv, mlp_ln_g, mlp_ln_b, mlp_w_s, mlp_b_st, layer):
    r, c2 = uv.shape
    c = c2 // 2
    t = mlp_w_s.shape[2]
    return pl.pallas_call(
        _spatial_first_body,
        grid=(1,),
        in_specs=[pl.BlockSpec((r, c), lambda i: (0, 0)),
                  pl.BlockSpec((r, c), lambda i: (0, 1)),
                  pl.BlockSpec((None, 1, c), lambda i: (layer, 0, 0)),
                  pl.BlockSpec((None, 1, c), lambda i: (layer, 0, 0)),
                  pl.BlockSpec((None, MLP_GROUPS, t, t), lambda i: (layer, 0, 0, 0)),
                  pl.BlockSpec((None, t, MLP_GROUPS), lambda i: (layer, 0, 0))],
        out_specs=[pl.BlockSpec((r, c), lambda i: (0, 0)), pl.BlockSpec((r, c), lambda i: (0, 0))],
        out_shape=[jax.ShapeDtypeStruct((r, c), BF16), jax.ShapeDtypeStruct((r, c), F32)],
        compiler_params=_params("arbitrary"),
        name="spatial_first",
    )(uv, uv, mlp_ln_g, mlp_ln_b, mlp_w_s, mlp_b_st)


def _conv_body(cb_ref, cc_ref, ch_ref, w_ref, o_ref, st_ref):
    z = cc_ref[...] * ch_ref[...]
    seq = z.shape[0]
    row = lax.broadcasted_iota(jnp.int32, z.shape, 0)
    z1 = jnp.where(row >= 1, pltpu.roll(z, 1, 0), 0.0)
    z2 = jnp.where(row >= 2, pltpu.roll(z, 2, 0), 0.0)
    y = w_ref[0:1, :] * z2 + w_ref[1:2, :] * z1 + w_ref[2:3, :] * z
    o_ref[...] = (cb_ref[...] * y).astype(o_ref.dtype)
    st_ref[...] = z[seq - (CONV_W - 1):, :]


def _conv_prompt(cz, conv_w, layer, batch):
    r, c3 = cz.shape
    c = c3 // 3
    seq = r // batch
    tc = _tile(c, 256)
    nb = c // tc
    cz3 = cz.reshape(batch, seq, c3)
    blk = lambda off: pl.BlockSpec((None, seq, tc), lambda b, j: (b, 0, off + j))
    y, st = pl.pallas_call(
        _conv_body,
        grid=(batch, nb),
        in_specs=[blk(0), blk(nb), blk(2 * nb),
                  pl.BlockSpec((None, CONV_W, tc), lambda b, j: (layer, 0, j))],
        out_specs=[pl.BlockSpec((None, seq, tc), lambda b, j: (b, 0, j)),
                   pl.BlockSpec((None, CONV_W - 1, tc), lambda b, j: (b, 0, j))],
        out_shape=[jax.ShapeDtypeStruct((batch, seq, c), BF16),
                   jax.ShapeDtypeStruct((batch, CONV_W - 1, c), F32)],
        compiler_params=_params("arbitrary", "arbitrary"),
        name="conv",
    )(cz3, cz3, cz3, conv_w)
    return y.reshape(r, c), st


def _conv_step_body(cb_ref, cc_ref, ch_ref, prev_ref, w_ref, o_ref, st_ref):
    z = cc_ref[...] * ch_ref[...]
    c = z.shape[1]
    p0 = prev_ref[:, :c]
    p1 = prev_ref[:, c:]
    y = w_ref[0:1, :] * p0 + w_ref[1:2, :] * p1 + w_ref[2:3, :] * z
    o_ref[...] = (cb_ref[...] * y).astype(o_ref.dtype)
    st_ref[:, :c] = p1
    st_ref[:, c:] = z


def _conv_step(cz, state_flat, conv_w, layer):
    r, c3 = cz.shape
    c = c3 // 3
    blk = lambda j: pl.BlockSpec((r, c), lambda i: (0, j))
    y, st = pl.pallas_call(
        _conv_step_body,
        grid=(1,),
        in_specs=[blk(0), blk(1), blk(2),
                  pl.BlockSpec((None, r, (CONV_W - 1) * c), lambda i: (layer, 0, 0)),
                  pl.BlockSpec((None, CONV_W, c), lambda i: (layer, 0, 0))],
        out_specs=[pl.BlockSpec((r, c), lambda i: (0, 0)),
                   pl.BlockSpec((r, (CONV_W - 1) * c), lambda i: (0, 0))],
        out_shape=[jax.ShapeDtypeStruct((r, c), BF16),
                   jax.ShapeDtypeStruct((r, (CONV_W - 1) * c), F32)],
        compiler_params=_params("arbitrary"),
        name="conv_step",
    )(cz, cz, cz, state_flat, conv_w)
    return y, st.reshape(r, CONV_W - 1, c)


def _branch_body(*refs):
    yp_refs, ys_refs = refs[0:3], refs[3:6]
    w_ref = refs[6]
    gp_refs, gs_refs = refs[7:10], refs[10:13]
    o_ref, os_ref, wb_ref = refs[13:16]

    def merged(y_refs, g_refs):
        total = None
        for t in range(N_BRANCH):
            term = g_refs[t][...] * jnp.dot(y_refs[t][...], wb_ref[t], preferred_element_type=F32)
            total = term if total is None else total + term
        return total

    @pl.when(pl.program_id(1) == 0)
    def _():
        for t in range(N_BRANCH):
            k = w_ref.shape[1]
            _cast_weight(w_ref.at[t], wb_ref, t, _tile(k, 256, SUBLANE))
        os_ref[...] = merged(ys_refs, gs_refs).astype(os_ref.dtype)

    o_ref[...] = merged(yp_refs, gp_refs).astype(o_ref.dtype)


def _branch_merge(ys, gates, w_branch, layer):
    r, k = ys[0][0].shape
    rs = ys[0][1].shape[0]
    d = w_branch.shape[3]
    tn = _tile(d, 512)
    tm = _tile(r, 512, SUBLANE)
    nb = d // tn
    yp_spec = pl.BlockSpec((tm, k), lambda j, i: (i, 0))
    ys_spec = pl.BlockSpec((rs, k), lambda j, i: (0, 0))
    gp_spec = lambda t: pl.BlockSpec((tm, tn), lambda j, i: (i, t * nb + j))
    gs_spec = lambda t: pl.BlockSpec((rs, tn), lambda j, i: (0, t * nb + j))
    return pl.pallas_call(
        _branch_body,
        grid=(nb, r // tm),
        in_specs=[yp_spec] * 3 + [ys_spec] * 3
                 + [pl.BlockSpec((None, N_BRANCH, k, tn), lambda j, i: (layer, 0, 0, j))]
                 + [gp_spec(t) for t in range(3)] + [gs_spec(t) for t in range(3)],
        out_specs=[pl.BlockSpec((tm, tn), lambda j, i: (i, j)), pl.BlockSpec((rs, tn), lambda j, i: (0, j))],
        out_shape=[jax.ShapeDtypeStruct((r, d), BF16), jax.ShapeDtypeStruct((rs, d), BF16)],
        scratch_shapes=[pltpu.VMEM((N_BRANCH, k, tn), BF16)],
        compiler_params=_params("arbitrary", "arbitrary"),
        name="branch_merge",
    )(*[y[0] for y in ys], *[y[1] for y in ys], w_branch, *([gates[0]] * 3), *([gates[1]] * 3))


def _rope_tables(pos):
    half = HEAD_DIM // 2
    inv = ROPE_THETA ** (-jnp.arange(half, dtype=F32) * 2.0 / HEAD_DIM)
    ang = pos.astype(F32)[:, None] * inv[None, :]
    cos, sin = jnp.cos(ang), jnp.sin(ang)
    return jnp.concatenate([cos, cos], axis=-1), jnp.concatenate([-sin, sin], axis=-1)


def _layer(layer, x, xb, p, rope_p, rope_s, w, row_ops):
    d = x[0].shape[1]
    rs = x[1].shape[0]
    depth = w["ln_g"].shape[0]
    alpha = (2 * depth) ** 0.25
    att = w["n_heads"] * 2 * HEAD_DIM
    mlp_ch = w["mlp_ln_g"].shape[2]
    conv_ch = w["conv_w"].shape[2]
    ln = (w["ln_g"], w["ln_b"], layer)

    y = _ffn_out(_ffn_in(xb, w["ffn1_w_in"], layer), w["ffn1_w_out"], layer)
    x, xb = _res_ln(x, y, *ln, 0, alpha, 0.5)

    tm = _row_tile(rope_p[0].shape[0])
    n_seq_blocks = rope_p[0].shape[0] // tm
    rope = dict(xp=rope_p, xp_specs=[pl.BlockSpec((tm, HEAD_DIM), lambda j, i: (i % n_seq_blocks, 0))] * 2,
                xs=rope_s, xs_specs=[pl.BlockSpec((rs, HEAD_DIM), lambda j, i: (0, 0))] * 2, tm=tm)
    wmi = w["w_mix_in"]
    off = 0
    q = _proj(xb, wmi, layer, off, att, _epi_rope, F32, **rope); off += att
    k = _proj(xb, wmi, layer, off, att, _epi_rope, F32, **rope); off += att
    v = _proj(xb, wmi, layer, off, att, _epi_plain, F32); off += att
    uv = _proj(xb, wmi, layer, off, 2 * mlp_ch, _epi_gelu, F32); off += 2 * mlp_ch
    cz = _proj(xb, wmi, layer, off, 3 * conv_ch, _epi_plain, F32); off += 3 * conv_ch
    gates = _proj(xb, wmi, layer, off, N_BRANCH * d, _epi_sigmoid, F32)

    lam_init = 0.8 - 0.6 * math.exp(-0.3 * layer)
    y_att = row_ops["attend"](layer, q, k, v, lam_init)
    y_mlp, vn_s = row_ops["spatial"](layer, uv)
    y_conv, conv_state = row_ops["conv"](layer, cz)

    mixed = _branch_merge((y_att, y_mlp, y_conv), gates, w["w_branch"], layer)
    y = _proj(mixed, w["w_mix_out"], layer, 0, d, _epi_plain, F32)
    x, xb = _res_ln(x, y, *ln, 1, alpha, 1.0)

    y = _ffn_out(_ffn_in(xb, w["ffn2_w_in"], layer), w["ffn2_w_out"], layer)
    x, xb = _res_ln(x, y, *ln, 2, alpha, 0.5)

    x, xb = _ple(xb, x, p, w["ple_w_gate"], w["ple_w_proj"], layer)
    return x, xb, k, v, conv_state, vn_s


def _cast_body(x_ref, o_ref):
    o_ref[...] = x_ref[...].astype(o_ref.dtype)


def _to_bf16(x):
    r, d = x.shape
    tm = _tile(r, 512, SUBLANE)
    spec = pl.BlockSpec((tm, d), lambda i: (i, 0))
    return pl.pallas_call(
        _cast_body, grid=(r // tm,), in_specs=[spec], out_specs=spec,
        out_shape=jax.ShapeDtypeStruct((r, d), BF16), compiler_params=_params("arbitrary"),
        name="to_bf16",
    )(x)


def kernel(x_prompt, x_sample, cache_k, cache_v, state_conv, page_table, p_prompt, p_sample,
           ln_g, ln_b, ffn1_w_in, ffn1_w_out, w_mix_in, att_lambda, att_subln_g, mlp_ln_g, mlp_ln_b,
           mlp_w_s, mlp_b_s, conv_w, w_branch, w_mix_out, ffn2_w_in, ffn2_w_out, ple_w_gate, ple_w_proj):
    bp, lp, d = x_prompt.shape
    bs, ls, _ = x_sample.shape
    assert ls == 1, "the sample group advances one position per step"
    depth = ln_g.shape[0]
    n_heads = cache_k.shape[3]
    hw = 2 * HEAD_DIM
    past = page_table.shape[1] * PAGE_SIZE
    rp, rs = bp * lp, bs * ls

    w = dict(
        ln_g=ln_g, ln_b=ln_b, ffn1_w_in=ffn1_w_in, ffn1_w_out=ffn1_w_out, w_mix_in=w_mix_in,
        mlp_ln_g=mlp_ln_g.reshape(depth, 1, -1), mlp_ln_b=mlp_ln_b.reshape(depth, 1, -1), conv_w=conv_w,
        w_branch=w_branch, w_mix_out=w_mix_out, ffn2_w_in=ffn2_w_in, ffn2_w_out=ffn2_w_out,
        ple_w_gate=ple_w_gate, ple_w_proj=ple_w_proj, n_heads=n_heads)
    subln_g = att_subln_g.reshape(depth, 1, hw)
    subln_gt = att_subln_g.reshape(depth, hw, 1)
    mlp_b_st = jnp.swapaxes(mlp_b_s, 1, 2)
    state_flat = state_conv.reshape(depth, bs, -1)

    rope_p = _rope_tables(jnp.arange(lp))
    rope_s = tuple(jnp.broadcast_to(t, (rs, HEAD_DIM)) for t in _rope_tables(past + jnp.arange(ls)))

    def attend(layer, q, k, v, lam_init):
        shp = (bp, lp, n_heads * hw)
        yp = _prompt_attn(q[0].reshape(shp), k[0].reshape(shp), v[0].reshape(shp), att_lambda, subln_gt,
                          layer, lam_init).reshape(rp, n_heads * hw)
        shs = (rs, n_heads, hw)
        ys = _sample_attn(q[1].reshape(shs), k[1].reshape(shs), v[1].reshape(shs), cache_k, cache_v, page_table,
                          att_lambda, subln_g, layer, lam_init).reshape(rs, n_heads * hw)
        return yp, ys

    def spatial(layer, uv):
        yp = _spatial(uv[0], w["mlp_ln_g"], w["mlp_ln_b"], mlp_w_s, mlp_b_st, layer)
        ys, vn_s = _spatial_first(uv[1], w["mlp_ln_g"], w["mlp_ln_b"], mlp_w_s, mlp_b_st, layer)
        return (yp, ys), vn_s

    def conv(layer, cz):
        yp, cp = _conv_prompt(cz[0], conv_w, layer, bp)
        ys, cs = _conv_step(cz[1], state_flat, conv_w, layer)
        return (yp, ys), (cp, cs)

    row_ops = dict(attend=attend, spatial=spatial, conv=conv)

    x = (x_prompt.reshape(rp, d), x_sample.reshape(rs, d))
    xb = (_to_bf16(x[0]), _to_bf16(x[1]))
    p = (p_prompt.reshape(depth, rp, -1), p_sample.reshape(depth, rs, -1))

    kp_l, vp_l, cp_l, ks_l, vs_l, cs_l, chs_l = [], [], [], [], [], [], []
    for layer in range(depth):
        x, xb, k, v, conv_state, vn_s = _layer(layer, x, xb, p, rope_p, rope_s, w, row_ops)
        kp_l.append(k[0].reshape(bp, lp, n_heads, hw)); vp_l.append(v[0].reshape(bp, lp, n_heads, hw))
        ks_l.append(k[1].reshape(bs, ls, n_heads, hw)); vs_l.append(v[1].reshape(bs, ls, n_heads, hw))
        cp_l.append(conv_state[0]); cs_l.append(conv_state[1]); chs_l.append(vn_s.reshape(bs, ls, -1))

    return (x[0].reshape(bp, lp, d), x[1].reshape(bs, ls, d), jnp.stack(kp_l), jnp.stack(vp_l), jnp.stack(cp_l),
            jnp.stack(ks_l), jnp.stack(vs_l), jnp.stack(cs_l), jnp.stack(chs_l))
```

```python
import functools
import math

import jax
import jax.numpy as jnp
from jax import lax
from jax.experimental import pallas as pl
from jax.experimental.pallas import tpu as pltpu

F32 = jnp.float32
BF16 = jnp.bfloat16

HEAD_DIM = 128
PAGE_SIZE = 128
MLP_GROUPS = 8
CONV_W = 3
N_BRANCH = 3
LN_EPS = 1e-5
ROPE_THETA = 10000.0

VMEM_LIMIT_BYTES = 56 * 1024 * 1024
LANE = 128
SUBLANE = 8

SOFTMAX_EXP2_SCALE = HEAD_DIM ** -0.5 * math.log2(math.e)
NEG = float(jnp.finfo(jnp.float32).min)
NT_DIMS = (((1,), (1,)), ((), ()))


def _tile(n, pref, mult=LANE):
    t = (min(pref, n) // mult) * mult
    while t >= mult:
        if n % t == 0:
            return t
        t -= mult
    return n


def _params(*sem):
    return pltpu.CompilerParams(dimension_semantics=sem, vmem_limit_bytes=VMEM_LIMIT_BYTES)


def _sigmoid(x):
    return 1.0 / (1.0 + jnp.exp(-x))


def _gelu_tanh(x):
    c = math.sqrt(2.0 / math.pi)
    return 0.5 * x * (1.0 + jnp.tanh(c * (x + 0.044715 * (x * x * x))))


def _cast_weight(w_ref, wb_ref, t, rows_per_step):
    k = w_ref.shape[0]
    steps = k // rows_per_step

    def body(r, carry):
        rows = pl.ds(pl.multiple_of(r * rows_per_step, rows_per_step), rows_per_step)
        wb_ref[t, rows, :] = w_ref[rows, :].astype(BF16)
        return carry

    lax.fori_loop(0, steps, body, 0)


def _mm_body(*refs, n_w, n_xp, n_xs, n_sh, n_o, epi):
    it = iter(refs)
    a_ref, as_ref = next(it), next(it)
    w_refs = [next(it) for _ in range(n_w)]
    xp_refs = [next(it) for _ in range(n_xp)]
    xs_refs = [next(it) for _ in range(n_xs)]
    sh_refs = [next(it) for _ in range(n_sh)]
    op_refs = [next(it) for _ in range(n_o)]
    os_refs = [next(it) for _ in range(n_o)]
    wb_ref = next(it)

    def run(a, x_refs, o_refs):
        accs = [jnp.dot(a, wb_ref[t], preferred_element_type=F32) for t in range(n_w)]
        epi(accs, x_refs, o_refs)

    @pl.when(pl.program_id(1) == 0)
    def _():
        for t in range(n_w):
            k = w_refs[t].shape[0]
            _cast_weight(w_refs[t], wb_ref, t, _tile(k, 256, SUBLANE))
        run(as_ref[...], xs_refs + sh_refs, os_refs)

    run(a_ref[...], xp_refs + sh_refs, op_refs)


def _mm_call(a, a_s, ws, w_specs, xp, xp_specs, xs, xs_specs, sh, sh_specs, out_dtypes, n_cols, epi, *, tm, tn):
    r, k = a.shape
    rs = a_s.shape[0]
    n_o = len(out_dtypes)
    body = functools.partial(_mm_body, n_w=len(ws), n_xp=len(xp), n_xs=len(xs), n_sh=len(sh), n_o=n_o, epi=epi)
    outs = pl.pallas_call(
        body,
        grid=(n_cols // tn, r // tm),
        in_specs=[pl.BlockSpec((tm, k), lambda j, i: (i, 0)),
                  pl.BlockSpec((rs, k), lambda j, i: (0, 0)),
                  *w_specs, *xp_specs, *xs_specs, *sh_specs],
        out_specs=[pl.BlockSpec((tm, tn), lambda j, i: (i, j))] * n_o
                  + [pl.BlockSpec((rs, tn), lambda j, i: (0, j))] * n_o,
        out_shape=[jax.ShapeDtypeStruct((r, n_cols), dt) for dt in out_dtypes]
                  + [jax.ShapeDtypeStruct((rs, n_cols), dt) for dt in out_dtypes],
        scratch_shapes=[pltpu.VMEM((len(ws), k, tn), BF16)],
        compiler_params=_params("arbitrary", "arbitrary"),
        name="mm" + epi.__name__[len("_epi"):],
    )(a, a_s, *ws, *xp, *xs, *sh)
    return outs[:n_o], outs[n_o:]


def _row_tile(r):
    return _tile(r, 1024, SUBLANE)


def _w_spec(layer, k, tn, off_blocks):
    return pl.BlockSpec((None, k, tn), lambda j, i: (layer, 0, off_blocks + j))


def _epi_swiglu(accs, x_refs, o_refs):
    a, b = accs
    o_refs[0][...] = (a * _sigmoid(a) * b).astype(o_refs[0].dtype)


def _epi_plain(accs, x_refs, o_refs):
    o_refs[0][...] = accs[0].astype(o_refs[0].dtype)


def _epi_gelu(accs, x_refs, o_refs):
    o_refs[0][...] = _gelu_tanh(accs[0]).astype(o_refs[0].dtype)


def _epi_sigmoid(accs, x_refs, o_refs):
    o_refs[0][...] = _sigmoid(accs[0]).astype(o_refs[0].dtype)


def _epi_rope(accs, x_refs, o_refs):
    cos = x_refs[0][...]
    sin = x_refs[1][...]
    acc = accs[0]
    for c in range(acc.shape[1] // HEAD_DIM):
        cols = slice(c * HEAD_DIM, (c + 1) * HEAD_DIM)
        x = acc[:, cols]
        o_refs[0][:, cols] = (x * cos + pltpu.roll(x, HEAD_DIM // 2, 1) * sin).astype(o_refs[0].dtype)


def _epi_ple(accs, x_refs, o_refs):
    x_ref, p_ref, pp_ref = x_refs
    proj = jnp.dot(p_ref[...].astype(BF16), pp_ref[...].astype(BF16), preferred_element_type=F32)
    out = x_ref[...] + _sigmoid(accs[0]) * proj
    o_refs[0][...] = out
    o_refs[1][...] = out.astype(BF16)


def _ffn_in(xb, w_in, layer):
    xp, xs = xb
    d = xp.shape[1]
    f = w_in.shape[2] // 2
    tn = _tile(f, 256)
    nb = f // tn
    (hp,), (hs,) = _mm_call(
        xp, xs, [w_in, w_in], [_w_spec(layer, d, tn, 0), _w_spec(layer, d, tn, nb)],
        [], [], [], [], [], [], [BF16], f, _epi_swiglu, tm=_row_tile(xp.shape[0]), tn=tn)
    return hp, hs


def _proj(xb, w, layer, col_off, n_cols, epi, out_dtype, xp=(), xp_specs=(), xs=(), xs_specs=(), tm=None):
    a, a_s = xb
    k = a.shape[1]
    tn = _tile(math.gcd(n_cols, col_off), 512)
    assert col_off % tn == 0 and n_cols % tn == 0
    tm = _row_tile(a.shape[0]) if tm is None else tm
    (op,), (os_,) = _mm_call(
        a, a_s, [w], [_w_spec(layer, k, tn, col_off // tn)],
        list(xp), list(xp_specs), list(xs), list(xs_specs), [], [],
        [out_dtype], n_cols, epi, tm=tm, tn=tn)
    return op, os_


def _ple(xb, x, p, w_gate, w_proj, layer):
    (ap, as_), (xp, xs), (pp, ps) = xb, x, p
    r, d = ap.shape
    rs = as_.shape[0]
    e = pp.shape[-1]
    tn = _tile(d, 512)
    tm = _tile(r, 512, SUBLANE)
    (of, ob), (sf, sb) = _mm_call(
        ap, as_, [w_gate], [_w_spec(layer, d, tn, 0)],
        [xp, pp], [pl.BlockSpec((tm, tn), lambda j, i: (i, j)),
                   pl.BlockSpec((None, tm, e), lambda j, i: (layer, i, 0))],
        [xs, ps], [pl.BlockSpec((rs, tn), lambda j, i: (0, j)),
                   pl.BlockSpec((None, rs, e), lambda j, i: (layer, 0, 0))],
        [w_proj], [pl.BlockSpec((None, e, tn), lambda j, i: (layer, 0, j))],
        [F32, BF16], d, _epi_ple, tm=tm, tn=tn)
    return (of, sf), (ob, sb)


def _ffn_out_body(h_ref, hs_ref, w_ref, o_ref, os_ref):
    first_rows = pl.program_id(0) == 0

    @pl.when(pl.program_id(1) == 0)
    def _():
        o_ref[...] = jnp.zeros(o_ref.shape, F32)

    @pl.when(first_rows & (pl.program_id(1) == 0))
    def _():
        os_ref[...] = jnp.zeros(os_ref.shape, F32)

    h = h_ref[...]
    d = o_ref.shape[1]
    tc = _tile(d, 1024)
    for c in range(d // tc):
        cols = slice(c * tc, (c + 1) * tc)
        o_ref[:, cols] += jnp.dot(h, w_ref[:, cols].astype(BF16), preferred_element_type=F32)

    @pl.when(first_rows)
    def _():
        os_ref[...] += jnp.dot(hs_ref[...], w_ref[...].astype(BF16), preferred_element_type=F32)


def _ffn_out(h, w_out, layer):
    hp, hs = h
    r, f = hp.shape
    rs = hs.shape[0]
    d = w_out.shape[2]
    tk = _tile(f, 256)
    tm = _row_tile(r)
    return pl.pallas_call(
        _ffn_out_body,
        grid=(r // tm, f // tk),
        in_specs=[pl.BlockSpec((tm, tk), lambda i, k: (i, k)),
                  pl.BlockSpec((rs, tk), lambda i, k: (0, k)),
                  pl.BlockSpec((None, tk, d), lambda i, k: (layer, k, 0))],
        out_specs=[pl.BlockSpec((tm, d), lambda i, k: (i, 0)),
                   pl.BlockSpec((rs, d), lambda i, k: (0, 0))],
        out_shape=[jax.ShapeDtypeStruct((r, d), F32), jax.ShapeDtypeStruct((rs, d), F32)],
        compiler_params=_params("arbitrary", "arbitrary"),
        name="ffn_out",
    )(hp, hs, w_out)


def _ln_body(x_ref, y_ref, g_ref, b_ref, of_ref, ob_ref, *, alpha, s, which):
    z = alpha * x_ref[...] + s * y_ref[...]
    mu = jnp.mean(z, axis=-1, keepdims=True)
    zc = z - mu
    var = jnp.mean(zc * zc, axis=-1, keepdims=True)
    out = zc * lax.rsqrt(var + LN_EPS) * g_ref[which:which + 1, :] + b_ref[which:which + 1, :]
    of_ref[...] = out
    ob_ref[...] = out.astype(BF16)


def _res_ln_one(x, y, ln_g, ln_b, layer, which, alpha, s):
    r, d = x.shape
    tm = _tile(r, 256, SUBLANE)
    n_ln = ln_g.shape[1]
    body = functools.partial(_ln_body, alpha=alpha, s=s, which=which)
    row = pl.BlockSpec((tm, d), lambda i: (i, 0))
    par = pl.BlockSpec((None, n_ln, d), lambda i: (layer, 0, 0))
    return pl.pallas_call(
        body,
        grid=(r // tm,),
        in_specs=[row, row, par, par],
        out_specs=[row, row],
        out_shape=[jax.ShapeDtypeStruct((r, d), F32), jax.ShapeDtypeStruct((r, d), BF16)],
        compiler_params=_params("arbitrary"),
        name="res_ln",
    )(x, y, ln_g, ln_b)


def _res_ln(x, y, ln_g, ln_b, layer, which, alpha, s):
    pf, pb = _res_ln_one(x[0], y[0], ln_g, ln_b, layer, which, alpha, s)
    sf, sb = _res_ln_one(x[1], y[1], ln_g, ln_b, layer, which, alpha, s)
    return (pf, sf), (pb, sb)


def _lambda_value(lam_ref, lam_init):
    lf = lam_ref[...]
    a = jnp.sum(lf[0:1, :] * lf[1:2, :], axis=-1, keepdims=True)
    b = jnp.sum(lf[2:3, :] * lf[3:4, :], axis=-1, keepdims=True)
    return jnp.exp(a) - jnp.exp(b) + lam_init


def _attn_body(q_ref, k_ref, v_ref, lam_ref, gt_ref, o_ref, kb_ref, vt_ref, *, tq, lam_init):
    i = pl.program_id(2)
    n_blocks = k_ref.shape[0] // tq

    @pl.when(i == 0)
    def _():
        for r in range(n_blocks):
            rows = slice(r * tq, (r + 1) * tq)
            kb_ref[rows, :] = k_ref[rows, :].astype(BF16)
            vt_ref[:, rows] = v_ref[rows, :].T.astype(BF16)

    q = q_ref[...]
    qs = (q[:, :HEAD_DIM].astype(BF16), q[:, HEAD_DIM:].astype(BF16))
    lam = _lambda_value(lam_ref, lam_init)
    keep = lax.broadcasted_iota(jnp.int32, (tq, tq), 0) <= lax.broadcasted_iota(jnp.int32, (tq, tq), 1)

    def scores(c, rows):
        return lax.dot_general(kb_ref[rows, c * HEAD_DIM:(c + 1) * HEAD_DIM], qs[c], NT_DIMS,
                               preferred_element_type=F32)

    def probs(s, m):
        p = jnp.exp2((s - m) * SOFTMAX_EXP2_SCALE)
        return p.astype(BF16), jnp.sum(p, axis=0, keepdims=True)

    def attend(n_full):
        full = slice(0, n_full * tq)
        diag = slice(n_full * tq, (n_full + 1) * tq)
        outs = []
        for c in range(2):
            s_d = jnp.where(keep, scores(c, diag), NEG)
            m = jnp.max(s_d, axis=0, keepdims=True)
            if n_full:
                s_f = scores(c, full)
                m = jnp.maximum(m, jnp.max(s_f, axis=0, keepdims=True))
            p_d, l = probs(s_d, m)
            acc = jnp.dot(vt_ref[:, diag], p_d, preferred_element_type=F32)
            if n_full:
                p_f, l_f = probs(s_f, m)
                l = l + l_f
                acc = acc + jnp.dot(vt_ref[:, full], p_f, preferred_element_type=F32)
            outs.append(acc * (1.0 / l))
        ot = outs[0] - lam * outs[1]
        ms = jnp.mean(ot * ot, axis=0, keepdims=True)
        ot = ot * lax.rsqrt(ms + LN_EPS) * gt_ref[...] * (1.0 - lam_init)
        o_ref[...] = ot.T.astype(o_ref.dtype)

    for n_full in range(n_blocks):
        pl.when(i == n_full)(functools.partial(attend, n_full))


def _prompt_attn(q, k, v, att_lambda, subln_gt, layer, lam_init):
    b, seq, hv = q.shape
    hw = 2 * HEAD_DIM
    n_heads = hv // hw
    tq = _tile(seq, 256, LANE)
    body = functools.partial(_attn_body, tq=tq, lam_init=lam_init)
    kv_spec = pl.BlockSpec((None, seq, hw), lambda bb, h, i: (bb, 0, h))
    q_spec = pl.BlockSpec((None, tq, hw), lambda bb, h, i: (bb, i, h))
    return pl.pallas_call(
        body,
        grid=(b, n_heads, seq // tq),
        in_specs=[q_spec, kv_spec, kv_spec,
                  pl.BlockSpec((None, 4, HEAD_DIM), lambda bb, h, i: (layer, 0, 0)),
                  pl.BlockSpec((None, hw, 1), lambda bb, h, i: (layer, 0, 0))],
        out_specs=q_spec,
        out_shape=jax.ShapeDtypeStruct((b, seq, hv), BF16),
        scratch_shapes=[pltpu.VMEM((seq, hw), BF16), pltpu.VMEM((hw, seq), BF16)],
        compiler_params=_params("arbitrary", "arbitrary", "arbitrary"),
        name="prompt_attn",
    )(q, k, v, att_lambda, subln_gt)


def _sample_attn_body(pt_ref, q_ref, kn_ref, vn_ref, *rest, n_pp, lam_init):
    ck_refs, cv_refs = rest[:n_pp], rest[n_pp:2 * n_pp]
    lam_ref, g_ref, o_ref, m_ref, l_ref, acc_ref = rest[2 * n_pp:]
    step_id = pl.program_id(1)
    n_heads, hw = q_ref.shape
    rows = ck_refs[0].shape[0] * n_heads

    q = q_ref[...]
    lane = lax.broadcasted_iota(jnp.int32, (n_heads, hw), 1)
    qm = jnp.concatenate([jnp.where(lane < HEAD_DIM, q, 0.0), jnp.where(lane >= HEAD_DIM, q, 0.0)], axis=0)
    qmb = qm.astype(BF16)

    @pl.when(step_id == 0)
    def _():
        m_ref[...] = jnp.full(m_ref.shape, NEG, F32)
        l_ref[...] = jnp.zeros(l_ref.shape, F32)
        acc_ref[...] = jnp.zeros(acc_ref.shape, F32)

    head_mask = n_heads - 1
    own_head = ((lax.broadcasted_iota(jnp.int32, (2 * n_heads, rows), 1) & head_mask)
                == (lax.broadcasted_iota(jnp.int32, (2 * n_heads, rows), 0) & head_mask))

    scores = []
    for ck_ref in ck_refs:
        kk = ck_ref[...].reshape(rows, hw).astype(BF16)
        s = lax.dot_general(qmb, kk, NT_DIMS, preferred_element_type=F32)
        scores.append(jnp.where(own_head, s, NEG))
    m = m_ref[...]
    m_new = m
    for s in scores:
        m_new = jnp.maximum(m_new, jnp.max(s, axis=-1, keepdims=True))
    corr = jnp.exp2((m - m_new) * SOFTMAX_EXP2_SCALE)
    l = l_ref[...] * corr
    acc = acc_ref[...] * corr
    for s, cv_ref in zip(scores, cv_refs):
        pr = jnp.exp2((s - m_new) * SOFTMAX_EXP2_SCALE)
        vv = cv_ref[...].reshape(rows, hw).astype(BF16)
        l = l + jnp.sum(pr, axis=-1, keepdims=True)
        acc = acc + jnp.dot(pr.astype(BF16), vv, preferred_element_type=F32)
    m_ref[...] = m_new
    l_ref[...] = l
    acc_ref[...] = acc

    @pl.when(step_id == pl.num_programs(1) - 1)
    def _():
        kn = kn_ref[...]
        vn = vn_ref[...]
        s = jnp.sum(qm * jnp.concatenate([kn, kn], axis=0), axis=-1, keepdims=True)
        m = m_ref[...]
        m_new = jnp.maximum(m, s)
        corr = jnp.exp2((m - m_new) * SOFTMAX_EXP2_SCALE)
        pr = jnp.exp2((s - m_new) * SOFTMAX_EXP2_SCALE)
        l = l_ref[...] * corr + pr
        acc = acc_ref[...] * corr + pr * jnp.concatenate([vn, vn], axis=0)
        out = acc * (1.0 / l)
        o = out[:n_heads] - _lambda_value(lam_ref, lam_init) * out[n_heads:]
        ms = jnp.mean(o * o, axis=-1, keepdims=True)
        o_ref[...] = (o * lax.rsqrt(ms + LN_EPS) * g_ref[...] * (1.0 - lam_init)).astype(o_ref.dtype)


def _sample_attn(q, k_new, v_new, cache_k, cache_v, page_table, att_lambda, subln_g, layer, lam_init):
    b, n_heads, hw = q.shape
    assert n_heads & (n_heads - 1) == 0
    n_pages = page_table.shape[1]
    page = cache_k.shape[2]
    n_pp = 4 if n_pages % 4 == 0 else (2 if n_pages % 2 == 0 else 1)
    body = functools.partial(_sample_attn_body, n_pp=n_pp, lam_init=lam_init)
    row = pl.BlockSpec((None, n_heads, hw), lambda bb, p, pt: (bb, 0, 0))

    def cache(t):
        return pl.BlockSpec((None, None, page, n_heads, hw),
                            lambda bb, p, pt: (layer, pt[bb, p * n_pp + t], 0, 0, 0))

    caches = [cache(t) for t in range(n_pp)]
    grid_spec = pltpu.PrefetchScalarGridSpec(
        num_scalar_prefetch=1,
        grid=(b, n_pages // n_pp),
        in_specs=[row, row, row, *caches, *caches,
                  pl.BlockSpec((None, 4, HEAD_DIM), lambda bb, p, pt: (layer, 0, 0)),
                  pl.BlockSpec((None, 1, hw), lambda bb, p, pt: (layer, 0, 0))],
        out_specs=row,
        scratch_shapes=[pltpu.VMEM((2 * n_heads, 1), F32), pltpu.VMEM((2 * n_heads, 1), F32),
                        pltpu.VMEM((2 * n_heads, hw), F32)],
    )
    return pl.pallas_call(
        body,
        grid_spec=grid_spec,
        out_shape=jax.ShapeDtypeStruct((b, n_heads, hw), BF16),
        compiler_params=_params("arbitrary", "arbitrary"),
        name="sample_attn",
    )(page_table, q, k_new, v_new, *([cache_k] * n_pp), *([cache_v] * n_pp), att_lambda, subln_g)


def _mlp_norm(v, g_ref, b_ref):
    mu = jnp.mean(v, axis=-1, keepdims=True)
    vc = v - mu
    var = jnp.mean(vc * vc, axis=-1, keepdims=True)
    return vc * lax.rsqrt(var + LN_EPS) * g_ref[...] + b_ref[...]


def _spatial_body(u_ref, v_ref, g_ref, b_ref, ws_ref, bst_ref, o_ref):
    vn = _mlp_norm(v_ref[...], g_ref, b_ref).astype(BF16)
    t = ws_ref.shape[1]
    gc = vn.shape[1] // MLP_GROUPS
    causal = lax.broadcasted_iota(jnp.int32, (t, t), 1) <= lax.broadcasted_iota(jnp.int32, (t, t), 0)
    for g in range(MLP_GROUPS):
        cols = slice(g * gc, (g + 1) * gc)
        wm = jnp.where(causal, ws_ref[g], 0.0).astype(BF16)
        sp = jnp.dot(wm, vn[:, cols], preferred_element_type=F32) + bst_ref[:, g:g + 1]
        o_ref[:, cols] = (u_ref[:, cols] * sp).astype(o_ref.dtype)


def _spatial(uv, mlp_ln_g, mlp_ln_b, mlp_w_s, mlp_b_st, layer):
    r, c2 = uv.shape
    c = c2 // 2
    t = mlp_w_s.shape[2]
    return pl.pallas_call(
        _spatial_body,
        grid=(r // t,),
        in_specs=[pl.BlockSpec((t, c), lambda i: (i, 0)),
                  pl.BlockSpec((t, c), lambda i: (i, 1)),
                  pl.BlockSpec((None, 1, c), lambda i: (layer, 0, 0)),
                  pl.BlockSpec((None, 1, c), lambda i: (layer, 0, 0)),
                  pl.BlockSpec((None, MLP_GROUPS, t, t), lambda i: (layer, 0, 0, 0)),
                  pl.BlockSpec((None, t, MLP_GROUPS), lambda i: (layer, 0, 0))],
        out_specs=pl.BlockSpec((t, c), lambda i: (i, 0)),
        out_shape=jax.ShapeDtypeStruct((r, c), BF16),
        compiler_params=_params("arbitrary"),
        name="spatial",
    )(uv, uv, mlp_ln_g, mlp_ln_b, mlp_w_s, mlp_b_st)


def _spatial_first_body(u_ref, v_ref, g_ref, b_ref, ws_ref, bst_ref, o_ref, vn_ref):
    vn = _mlp_norm(v_ref[...], g_ref, b_ref)
    vn_ref[...] = vn
    gc = vn.shape[1] // MLP_GROUPS
    for g in range(MLP_GROUPS):
        cols = slice(g * gc, (g + 1) * gc)
        sp = ws_ref[g, 0:1, 0:1] * vn[:, cols] + bst_ref[0:1, g:g + 1]
        o_ref[:, cols] = (u_ref[:, cols] * sp).astype(o_ref.dtype)


def _spatial_first(uv, mlp_ln_g, mlp_ln_b, mlp_w_s, mlp_b_st, layer):
    r, c2 = uv.shape
    c = c2 // 2
    t = mlp_w_s.shape[2]
    return pl.pallas_call(
        _spatial_first_body,
        grid=(1,),
        in_specs=[pl.BlockSpec((r, c), lambda i: (0, 0)),
                  pl.BlockSpec((r, c), lambda i: (0, 1)),
                  pl.BlockSpec((None, 1, c), lambda i: (layer, 0, 0)),
                  pl.BlockSpec((None, 1, c), lambda i: (layer, 0, 0)),
                  pl.BlockSpec((None, MLP_GROUPS, t, t), lambda i: (layer, 0, 0, 0)),
                  pl.BlockSpec((None, t, MLP_GROUPS), lambda i: (layer, 0, 0))],
        out_specs=[pl.BlockSpec((r, c), lambda i: (0, 0)), pl.BlockSpec((r, c), lambda i: (0, 0))],
        out_shape=[jax.ShapeDtypeStruct((r, c), BF16), jax.ShapeDtypeStruct((r, c), F32)],
        compiler_params=_params("arbitrary"),
        name="spatial_first",
    )(uv, uv, mlp_ln_g, mlp_ln_b, mlp_w_s, mlp_b_st)


def _conv_body(cb_ref, cc_ref, ch_ref, w_ref, o_ref, st_ref):
    z = cc_ref[...] * ch_ref[...]
    seq = z.shape[0]
    row = lax.broadcasted_iota(jnp.int32, z.shape, 0)
    z1 = jnp.where(row >= 1, pltpu.roll(z, 1, 0), 0.0)
    z2 = jnp.where(row >= 2, pltpu.roll(z, 2, 0), 0.0)
    y = w_ref[0:1, :] * z2 + w_ref[1:2, :] * z1 + w_ref[2:3, :] * z
    o_ref[...] = (cb_ref[...] * y).astype(o_ref.dtype)
    st_ref[...] = z[seq - (CONV_W - 1):, :]


def _conv_prompt(cz, conv_w, layer, batch):
    r, c3 = cz.shape
    c = c3 // 3
    seq = r // batch
    tc = _tile(c, 256)
    nb = c // tc
    cz3 = cz.reshape(batch, seq, c3)
    blk = lambda off: pl.BlockSpec((None, seq, tc), lambda b, j: (b, 0, off + j))
    y, st = pl.pallas_call(
        _conv_body,
        grid=(batch, nb),
        in_specs=[blk(0), blk(nb), blk(2 * nb),
                  pl.BlockSpec((None, CONV_W, tc), lambda b, j: (layer, 0, j))],
        out_specs=[pl.BlockSpec((None, seq, tc), lambda b, j: (b, 0, j)),
                   pl.BlockSpec((None, CONV_W - 1, tc), lambda b, j: (b, 0, j))],
        out_shape=[jax.ShapeDtypeStruct((batch, seq, c), BF16),
                   jax.ShapeDtypeStruct((batch, CONV_W - 1, c), F32)],
        compiler_params=_params("arbitrary", "arbitrary"),
        name="conv",
    )(cz3, cz3, cz3, conv_w)
    return y.reshape(r, c), st


def _conv_step_body(cb_ref, cc_ref, ch_ref, prev_ref, w_ref, o_ref, st_ref):
    z = cc_ref[...] * ch_ref[...]
    c = z.shape[1]
    p0 = prev_ref[:, :c]
    p1 = prev_ref[:, c:]
    y = w_ref[0:1, :] * p0 + w_ref[1:2, :] * p1 + w_ref[2:3, :] * z
    o_ref[...] = (cb_ref[...] * y).astype(o_ref.dtype)
    st_ref[:, :c] = p1
    st_ref[:, c:] = z


def _conv_step(cz, state_flat, conv_w, layer):
    r, c3 = cz.shape
    c = c3 // 3
    blk = lambda j: pl.BlockSpec((r, c), lambda i: (0, j))
    y, st = pl.pallas_call(
        _conv_step_body,
        grid=(1,),
        in_specs=[blk(0), blk(1), blk(2),
                  pl.BlockSpec((None, r, (CONV_W - 1) * c), lambda i: (layer, 0, 0)),
                  pl.BlockSpec((None, CONV_W, c), lambda i: (layer, 0, 0))],
        out_specs=[pl.BlockSpec((r, c), lambda i: (0, 0)),
                   pl.BlockSpec((r, (CONV_W - 1) * c), lambda i: (0, 0))],
        out_shape=[jax.ShapeDtypeStruct((r, c), BF16),
                   jax.ShapeDtypeStruct((r, (CONV_W - 1) * c), F32)],
        compiler_params=_params("arbitrary"),
        name="conv_step",
    )(cz, cz, cz, state_flat, conv_w)
    return y, st.reshape(r, CONV_W - 1, c)


def _branch_body(*refs):
    yp_refs, ys_refs = refs[0:3], refs[3:6]
    w_ref = refs[6]
    gp_refs, gs_refs = refs[7:10], refs[10:13]
    o_ref, os_ref, wb_ref = refs[13:16]

    def merged(y_refs, g_refs):
        total = None
        for t in range(N_BRANCH):
            term = g_refs[t][...] * jnp.dot(y_refs[t][...], wb_ref[t], preferred_element_type=F32)
            total = term if total is None else total + term
        return total

    @pl.when(pl.program_id(1) == 0)
    def _():
        for t in range(N_BRANCH):
            k = w_ref.shape[1]
            _cast_weight(w_ref.at[t], wb_ref, t, _tile(k, 256, SUBLANE))
        os_ref[...] = merged(ys_refs, gs_refs).astype(os_ref.dtype)

    o_ref[...] = merged(yp_refs, gp_refs).astype(o_ref.dtype)


def _branch_merge(ys, gates, w_branch, layer):
    r, k = ys[0][0].shape
    rs = ys[0][1].shape[0]
    d = w_branch.shape[3]
    tn = _tile(d, 512)
    tm = _tile(r, 512, SUBLANE)
    nb = d // tn
    yp_spec = pl.BlockSpec((tm, k), lambda j, i: (i, 0))
    ys_spec = pl.BlockSpec((rs, k), lambda j, i: (0, 0))
    gp_spec = lambda t: pl.BlockSpec((tm, tn), lambda j, i: (i, t * nb + j))
    gs_spec = lambda t: pl.BlockSpec((rs, tn), lambda j, i: (0, t * nb + j))
    return pl.pallas_call(
        _branch_body,
        grid=(nb, r // tm),
        in_specs=[yp_spec] * 3 + [ys_spec] * 3
                 + [pl.BlockSpec((None, N_BRANCH, k, tn), lambda j, i: (layer, 0, 0, j))]
                 + [gp_spec(t) for t in range(3)] + [gs_spec(t) for t in range(3)],
        out_specs=[pl.BlockSpec((tm, tn), lambda j, i: (i, j)), pl.BlockSpec((rs, tn), lambda j, i: (0, j))],
        out_shape=[jax.ShapeDtypeStruct((r, d), BF16), jax.ShapeDtypeStruct((rs, d), BF16)],
        scratch_shapes=[pltpu.VMEM((N_BRANCH, k, tn), BF16)],
        compiler_params=_params("arbitrary", "arbitrary"),
        name="branch_merge",
    )(*[y[0] for y in ys], *[y[1] for y in ys], w_branch, *([gates[0]] * 3), *([gates[1]] * 3))


def _rope_tables(pos):
    half = HEAD_DIM // 2
    inv = ROPE_THETA ** (-jnp.arange(half, dtype=F32) * 2.0 / HEAD_DIM)
    ang = pos.astype(F32)[:, None] * inv[None, :]
    cos, sin = jnp.cos(ang), jnp.sin(ang)
    return jnp.concatenate([cos, cos], axis=-1), jnp.concatenate([-sin, sin], axis=-1)


def _layer(layer, x, xb, p, rope_p, rope_s, w, row_ops):
    d = x[0].shape[1]
    rs = x[1].shape[0]
    depth = w["ln_g"].shape[0]
    alpha = (2 * depth) ** 0.25
    att = w["n_heads"] * 2 * HEAD_DIM
    mlp_ch = w["mlp_ln_g"].shape[2]
    conv_ch = w["conv_w"].shape[2]
    ln = (w["ln_g"], w["ln_b"], layer)

    y = _ffn_out(_ffn_in(xb, w["ffn1_w_in"], layer), w["ffn1_w_out"], layer)
    x, xb = _res_ln(x, y, *ln, 0, alpha, 0.5)

    tm = _row_tile(rope_p[0].shape[0])
    n_seq_blocks = rope_p[0].shape[0] // tm
    rope = dict(xp=rope_p, xp_specs=[pl.BlockSpec((tm, HEAD_DIM), lambda j, i: (i % n_seq_blocks, 0))] * 2,
                xs=rope_s, xs_specs=[pl.BlockSpec((rs, HEAD_DIM), lambda j, i: (0, 0))] * 2, tm=tm)
    wmi = w["w_mix_in"]
    off = 0
    q = _proj(xb, wmi, layer, off, att, _epi_rope, F32, **rope); off += att
    k = _proj(xb, wmi, layer, off, att, _epi_rope, F32, **rope); off += att
    v = _proj(xb, wmi, layer, off, att, _epi_plain, F32); off += att
    uv = _proj(xb, wmi, layer, off, 2 * mlp_ch, _epi_gelu, F32); off += 2 * mlp_ch
    cz = _proj(xb, wmi, layer, off, 3 * conv_ch, _epi_plain, F32); off += 3 * conv_ch
    gates = _proj(xb, wmi, layer, off, N_BRANCH * d, _epi_sigmoid, F32)

    lam_init = 0.8 - 0.6 * math.exp(-0.3 * layer)
    y_att = row_ops["attend"](layer, q, k, v, lam_init)
    y_mlp, vn_s = row_ops["spatial"](layer, uv)
    y_conv, conv_state = row_ops["conv"](layer, cz)

    mixed = _branch_merge((y_att, y_mlp, y_conv), gates, w["w_branch"], layer)
    y = _proj(mixed, w["w_mix_out"], layer, 0, d, _epi_plain, F32)
    x, xb = _res_ln(x, y, *ln, 1, alpha, 1.0)

    y = _ffn_out(_ffn_in(xb, w["ffn2_w_in"], layer), w["ffn2_w_out"], layer)
    x, xb = _res_ln(x, y, *ln, 2, alpha, 0.5)

    x, xb = _ple(xb, x, p, w["ple_w_gate"], w["ple_w_proj"], layer)
    return x, xb, k, v, conv_state, vn_s


def _cast_body(x_ref, o_ref):
    o_ref[...] = x_ref[...].astype(o_ref.dtype)


def _to_bf16(x):
    r, d = x.shape
    tm = _tile(r, 512, SUBLANE)
    spec = pl.BlockSpec((tm, d), lambda i: (i, 0))
    return pl.pallas_call(
        _cast_body, grid=(r // tm,), in_specs=[spec], out_specs=spec,
        out_shape=jax.ShapeDtypeStruct((r, d), BF16), compiler_params=_params("arbitrary"),
        name="to_bf16",
    )(x)


def kernel(x_prompt, x_sample, cache_k, cache_v, state_conv, page_table, p_prompt, p_sample,
           ln_g, ln_b, ffn1_w_in, ffn1_w_out, w_mix_in, att_lambda, att_subln_g, mlp_ln_g, mlp_ln_b,
           mlp_w_s, mlp_b_s, conv_w, w_branch, w_mix_out, ffn2_w_in, ffn2_w_out, ple_w_gate, ple_w_proj):
    bp, lp, d = x_prompt.shape
    bs, ls, _ = x_sample.shape
    assert ls == 1, "the sample group advances one position per step"
    depth = ln_g.shape[0]
    n_heads = cache_k.shape[3]
    hw = 2 * HEAD_DIM
    past = page_table.shape[1] * PAGE_SIZE
    rp, rs = bp * lp, bs * ls

    w = dict(
        ln_g=ln_g, ln_b=ln_b, ffn1_w_in=ffn1_w_in, ffn1_w_out=ffn1_w_out, w_mix_in=w_mix_in,
        mlp_ln_g=mlp_ln_g.reshape(depth, 1, -1), mlp_ln_b=mlp_ln_b.reshape(depth, 1, -1), conv_w=conv_w,
        w_branch=w_branch, w_mix_out=w_mix_out, ffn2_w_in=ffn2_w_in, ffn2_w_out=ffn2_w_out,
        ple_w_gate=ple_w_gate, ple_w_proj=ple_w_proj, n_heads=n_heads)
    subln_g = att_subln_g.reshape(depth, 1, hw)
    subln_gt = att_subln_g.reshape(depth, hw, 1)
    mlp_b_st = jnp.swapaxes(mlp_b_s, 1, 2)
    state_flat = state_conv.reshape(depth, bs, -1)

    rope_p = _rope_tables(jnp.arange(lp))
    rope_s = tuple(jnp.broadcast_to(t, (rs, HEAD_DIM)) for t in _rope_tables(past + jnp.arange(ls)))

    def attend(layer, q, k, v, lam_init):
        shp = (bp, lp, n_heads * hw)
        yp = _prompt_attn(q[0].reshape(shp), k[0].reshape(shp), v[0].reshape(shp), att_lambda, subln_gt,
                          layer, lam_init).reshape(rp, n_heads * hw)
        shs = (rs, n_heads, hw)
        ys = _sample_attn(q[1].reshape(shs), k[1].reshape(shs), v[1].reshape(shs), cache_k, cache_v, page_table,
                          att_lambda, subln_g, layer, lam_init).reshape(rs, n_heads * hw)
        return yp, ys

    def spatial(layer, uv):
        yp = _spatial(uv[0], w["mlp_ln_g"], w["mlp_ln_b"], mlp_w_s, mlp_b_st, layer)
        ys, vn_s = _spatial_first(uv[1], w["mlp_ln_g"], w["mlp_ln_b"], mlp_w_s, mlp_b_st, layer)
        return (yp, ys), vn_s

    def conv(layer, cz):
        yp, cp = _conv_prompt(cz[0], conv_w, layer, bp)
        ys, cs = _conv_step(cz[1], state_flat, conv_w, layer)
        return (yp, ys), (cp, cs)

    row_ops = dict(attend=attend, spatial=spatial, conv=conv)

    x = (x_prompt.reshape(rp, d), x_sample.reshape(rs, d))
    xb = (_to_bf16(x[0]), _to_bf16(x[1]))
    p = (p_prompt.reshape(depth, rp, -1), p_sample.reshape(depth, rs, -1))

    kp_l, vp_l, cp_l, ks_l, vs_l, cs_l, chs_l = [], [], [], [], [], [], []
    for layer in range(depth):
        x, xb, k, v, conv_state, vn_s = _layer(layer, x, xb, p, rope_p, rope_s, w, row_ops)
        kp_l.append(k[0].reshape(bp, lp, n_heads, hw)); vp_l.append(v[0].reshape(bp, lp, n_heads, hw))
        ks_l.append(k[1].reshape(bs, ls, n_heads, hw)); vs_l.append(v[1].reshape(bs, ls, n_heads, hw))
        cp_l.append(conv_state[0]); cs_l.append(conv_state[1]); chs_l.append(vn_s.reshape(bs, ls, -1))

    return (x[0].reshape(bp, lp, d), x[1].reshape(bs, ls, d), jnp.stack(kp_l), jnp.stack(vp_l), jnp.stack(cp_l),
            jnp.stack(ks_l), jnp.stack(vs_l), jnp.stack(cs_l), jnp.stack(chs_l))
```
